```python
import jax, jax.numpy as jnp
from jax import lax
import numpy as np

D_MODEL = 1024
BATCH = 16
SEQ = 2048
DEPTH = 1
DEC_BATCH = 128
DEC_SEQ = 8
PAST_LEN = 8192
PAGE_SIZE = 128

D_MIX = D_MODEL
D_A = D_MIX // 2
N_GROUPS_A = 4
GROUP_A = D_A // N_GROUPS_A
CHUNK = 128
D_B = D_MIX - D_A
N_HEADS_B = 8
HEAD_DIM = D_B // N_HEADS_B
PATTERNS = ((128, 1), (512, 4), (2048, 16))
WINDOW_MAX = 2048
BLK = 128
D_PLE = 256
EPS = 1e-6
NEG = -1e30
SCALE = HEAD_DIM ** -0.5
SPLITS = [D_A, 2 * D_A, 3 * D_A, 3 * D_A + D_B, 3 * D_A + 2 * D_B, 3 * D_A + 3 * D_B]
D_IN = 3 * D_A + 4 * D_B

kernel_name = "hymba_gmlp_dilated_attn_step"


def rms_norm(x, g):
    xf = x.astype(jnp.float32)
    y = xf * lax.rsqrt(jnp.mean(xf * xf, axis=-1, keepdims=True) + EPS)
    return (y * g.astype(jnp.float32)).astype(x.dtype)


def project(x, g_norm, w_in, g_q, g_k):
    B, S, _ = x.shape
    h = rms_norm(x, g_norm)
    proj = jnp.einsum('bsd,de->bse', h, w_in)
    u, va, za, q, k, vb, zb = jnp.split(proj, SPLITS, axis=-1)
    q = rms_norm(q.reshape(B, S, N_HEADS_B, HEAD_DIM), g_q)
    k = rms_norm(k.reshape(B, S, N_HEADS_B, HEAD_DIM), g_k)
    vb = vb.reshape(B, S, N_HEADS_B, HEAD_DIM)
    return u, va, za, q, k, vb, zb


def chunk_mlp(u, v, z, w_s, b_s, g_va, g_oa):
    B, L, _ = u.shape
    c = min(L, CHUNK)
    vn = rms_norm(v.reshape(B, L, N_GROUPS_A, GROUP_A), g_va)
    vc = vn.reshape(B, L // c, c, N_GROUPS_A, GROUP_A)
    ws = jnp.tril(w_s[:, :c, :c])
    mixed = jnp.einsum('gts,bnsgc->bntgc', ws, vc) + b_s[:, :c].T[None, None, :, :, None]
    a = u * mixed.reshape(B, L, D_A)
    a = rms_norm(a.reshape(B, L, N_GROUPS_A, GROUP_A), g_oa).reshape(B, L, D_A)
    return a * jax.nn.silu(z), vn.reshape(B, L, D_A)


def dilated_prompt_pattern(q, k, v, n, r):
    B, S, H, Dh = q.shape
    L = S // r
    nb = -(-L // BLK)
    Lp = nb * BLK

    def to_class(x):
        x = x.reshape(B, L, r, H, Dh).transpose(0, 2, 1, 3, 4)
        return jnp.pad(x, ((0, 0), (0, 0), (0, Lp - L), (0, 0), (0, 0)))

    def band(x):
        prev = jnp.pad(x, ((0, 0), (0, 0), (BLK, 0), (0, 0), (0, 0)))[:, :, :Lp]
        return jnp.concatenate([prev.reshape(B, r, nb, BLK, H, Dh),
                                x.reshape(B, r, nb, BLK, H, Dh)], axis=3)

    qb = to_class(q).reshape(B, r, nb, BLK, H, Dh).astype(jnp.float32)
    kb = band(to_class(k)).astype(jnp.float32)
    vb = band(to_class(v)).astype(jnp.float32)
    s = jnp.einsum('brnqhd,brnkhd->brnhqk', qb, kb) * SCALE
    qi = jnp.arange(BLK)[:, None]
    ki = jnp.arange(2 * BLK)[None, :]
    dist = BLK + qi - ki
    valid = ((dist >= 0) & (dist <= n))[None] & (
        (jnp.arange(nb)[:, None, None] > 0) | (ki >= BLK)[None])
    s = jnp.where(valid[None, None, :, None], s, NEG)
    m = jnp.max(s, axis=-1, keepdims=True)
    p = jnp.exp(s - m)
    den = jnp.sum(p, axis=-1, keepdims=True)
    o = jnp.einsum('brnhqk,brnkhd->brnhqd', p, vb) / den
    lse = (m + jnp.log(den))[..., 0]
    o = o.transpose(0, 1, 2, 4, 3, 5).reshape(B, r, Lp, H, Dh)[:, :, :L]
    o = o.transpose(0, 2, 1, 3, 4).reshape(B, S, H, Dh)
    lse = lse.transpose(0, 1, 2, 4, 3).reshape(B, r, Lp, H)[:, :, :L]
    lse = lse.transpose(0, 2, 1, 3).reshape(B, S, H)
    return o, lse


def dilated_sample_pattern(q, k_all, v_all, n, r):
    B, T, H, Dh = q.shape
    wb = k_all.shape[1] - T
    idx = wb + jnp.arange(T)[:, None] - r * jnp.arange(n + 1)[None, :]
    valid = idx >= 0
    idx = jnp.maximum(idx, 0)
    kg = jnp.take(k_all, idx, axis=1).astype(jnp.float32)
    vg = jnp.take(v_all, idx, axis=1).astype(jnp.float32)
    s = jnp.einsum('bthd,btkhd->bthk', q.astype(jnp.float32), kg) * SCALE
    s = jnp.where(valid[None, :, None, :], s, NEG)
    m = jnp.max(s, axis=-1, keepdims=True)
    p = jnp.exp(s - m)
    den = jnp.sum(p, axis=-1, keepdims=True)
    o = jnp.einsum('bthk,btkhd->bthd', p, vg) / den
    return o, (m + jnp.log(den))[..., 0]


def combine_by_denominator(outs, lses):
    w = jax.nn.softmax(jnp.stack(lses, axis=0), axis=0)
    return jnp.einsum('pbsh,pbshd->bshd', w, jnp.stack(outs, axis=0))


def finish(x, a_out, b_out, zb, g_ob, w_out, p, w_ple, g_ple, w_ple_gate):
    B, S, _ = x.shape
    b = rms_norm(b_out.astype(x.dtype), g_ob).reshape(B, S, D_B) * jax.nn.silu(zb)
    h = x + jnp.einsum('bse,ed->bsd', jnp.concatenate([a_out, b], axis=-1), w_out)
    gate = jax.nn.sigmoid(jnp.einsum('bsd,de->bse', h, w_ple_gate))
    e = rms_norm(jnp.einsum('bsp,pd->bsd', p, w_ple), g_ple)
    return h + gate * e


def setup_inputs(seed: int = 0) -> dict:
    key = jax.random.key(seed)
    ks = jax.random.split(key, 24)
    wb = min(WINDOW_MAX, PAST_LEN)
    f = jnp.float32

    def nrm(k, shape, scale=1.0):
        return jax.random.normal(k, shape, f) * scale

    def gain(k, shape):
        return 1.0 + 0.02 * jax.random.normal(k, shape, f)

    return {
        "x_prompt": nrm(ks[0], (BATCH, SEQ, D_MODEL)),
        "x_sample": nrm(ks[1], (DEC_BATCH, DEC_SEQ, D_MODEL)),
        "cache_k": nrm(ks[2], (DEPTH, DEC_BATCH, wb, N_HEADS_B, HEAD_DIM)),
        "cache_v": nrm(ks[3], (DEPTH, DEC_BATCH, wb, N_HEADS_B, HEAD_DIM)),
        "p_prompt": nrm(ks[4], (DEPTH, BATCH, SEQ, D_PLE)),
        "p_sample": nrm(ks[5], (DEPTH, DEC_BATCH, DEC_SEQ, D_PLE)),
        "g_norm": gain(ks[6], (DEPTH, D_MODEL)),
        "w_in": nrm(ks[7], (DEPTH, D_MODEL, D_IN), D_MODEL ** -0.5),
        "w_s": nrm(ks[8], (DEPTH, N_GROUPS_A, CHUNK, CHUNK), CHUNK ** -0.5),
        "b_s": 1.0 + 0.02 * nrm(ks[9], (DEPTH, N_GROUPS_A, CHUNK)),
        "g_va": gain(ks[10], (DEPTH, N_GROUPS_A, GROUP_A)),
        "g_oa": gain(ks[11], (DEPTH, N_GROUPS_A, GROUP_A)),
        "g_q": gain(ks[12], (DEPTH, HEAD_DIM)),
        "g_k": gain(ks[13], (DEPTH, HEAD_DIM)),
        "g_ob": gain(ks[14], (DEPTH, N_HEADS_B, HEAD_DIM)),
        "w_out": nrm(ks[15], (DEPTH, D_MIX, D_MODEL), D_MIX ** -0.5),
        "w_ple": nrm(ks[16], (DEPTH, D_PLE, D_MODEL), D_PLE ** -0.5),
        "g_ple": gain(ks[17], (DEPTH, D_MODEL)),
        "w_ple_gate": nrm(ks[18], (DEPTH, D_MODEL, D_MODEL), D_MODEL ** -0.5),
    }


def reference(x_prompt, x_sample, cache_k, cache_v, p_prompt, p_sample, g_norm, w_in, w_s,
              b_s, g_va, g_oa, g_q, g_k, g_ob, w_out, w_ple, g_ple, w_ple_gate):
    y_p, y_s = x_prompt, x_sample
    kp_list, vp_list, ks_list, vs_list, va_list = [], [], [], [], []
    for i in range(DEPTH):
        u, va, za, q, k, vb, zb = project(y_p, g_norm[i], w_in[i], g_q[i], g_k[i])
        a_out, _ = chunk_mlp(u, va, za, w_s[i], b_s[i], g_va[i], g_oa[i])
        outs, lses = [], []
        for (w, r) in PATTERNS:
            o, l = dilated_prompt_pattern(q, k, vb, w // r, r)
            outs.append(o)
            lses.append(l)
        b_out = combine_by_denominator(outs, lses)
        wb_p = min(WINDOW_MAX, y_p.shape[1])
        kp_list.append(k[:, -wb_p:])
        vp_list.append(vb[:, -wb_p:])
        y_p = finish(y_p, a_out, b_out, zb, g_ob[i], w_out[i], p_prompt[i], w_ple[i],
                     g_ple[i], w_ple_gate[i])

        u, va, za, q, k, vb, zb = project(y_s, g_norm[i], w_in[i], g_q[i], g_k[i])
        a_out, va_rows = chunk_mlp(u, va, za, w_s[i], b_s[i], g_va[i], g_oa[i])
        k_all = jnp.concatenate([cache_k[i].astype(k.dtype), k], axis=1)
        v_all = jnp.concatenate([cache_v[i].astype(vb.dtype), vb], axis=1)
        outs, lses = [], []
        for (w, r) in PATTERNS:
            o, l = dilated_sample_pattern(q, k_all, v_all, w // r, r)
            outs.append(o)
            lses.append(l)
        b_out = combine_by_denominator(outs, lses)
        ks_list.append(k)
        vs_list.append(vb)
        va_list.append(va_rows)
        y_s = finish(y_s, a_out, b_out, zb, g_ob[i], w_out[i], p_sample[i], w_ple[i],
                     g_ple[i], w_ple_gate[i])

    k_win_prompt = jnp.stack(kp_list, axis=0)
    v_win_prompt = jnp.stack(vp_list, axis=0)
    k_new_sample = jnp.stack(ks_list, axis=0)
    v_new_sample = jnp.stack(vs_list, axis=0)
    va_chunk_sample = jnp.stack(va_list, axis=0)
    return (y_p, y_s, k_win_prompt, v_win_prompt, k_new_sample, v_new_sample, va_chunk_sample)
```

```python
import functools

import numpy as np
import jax
import jax.numpy as jnp
from jax import lax
from jax.experimental import pallas as pl
from jax.experimental.pallas import tpu as pltpu

D_MODEL = 1024
D_A = 512
N_GROUPS_A = 4
GROUP_A = 128
CHUNK = 128
D_B = 512
N_HEADS_B = 8
HEAD_DIM = 64
PATTERNS = ((128, 1), (512, 4), (2048, 16))
D_PLE = 256
EPS = 1e-6
NEG = -1e30
SCALE = HEAD_DIM ** -0.5
SEG = 512
SEG_U, SEG_VA, SEG_ZA, SEG_Q, SEG_K, SEG_VB, SEG_ZB = range(7)

LANES = 128
VMEM_LIMIT = 48 * 1024 * 1024

TM_PROJ = 512
TM_FIN = 512
TQ = 256

BF16 = jnp.bfloat16
F32 = jnp.float32
NT = (((1,), (1,)), ((), ()))


def _multiplicity(d):
    c = np.zeros(d.shape, np.int32)
    for w, r in PATTERNS:
        c += ((d >= 0) & (d <= w) & (d % r == 0)).astype(np.int32)
    return c


def _log_bias(d):
    c = _multiplicity(d)
    return np.where(c > 0, np.log(np.maximum(c, 1).astype(np.float64)), NEG).astype(np.float32)


def _rsqrt_mean(x, n):
    return lax.rsqrt(jnp.sum(x * x, axis=-1, keepdims=True) * (1.0 / n) + EPS)


def _head_rms_scale(x):
    lane_lo = lax.broadcasted_iota(jnp.int32, x.shape, 1) < HEAD_DIM
    x2 = x * x
    lo = jnp.sum(jnp.where(lane_lo, x2, 0.0), axis=-1, keepdims=True)
    hi = jnp.sum(jnp.where(lane_lo, 0.0, x2), axis=-1, keepdims=True)
    ms = jnp.where(lane_lo, lo, hi) * (1.0 / HEAD_DIM)
    return lax.rsqrt(ms + EPS)


def _silu(z):
    return z * jax.nn.sigmoid(z)


def _full(shape):
    return pl.BlockSpec(shape, lambda *_: (0,) * len(shape))


def _proj_kernel(x_ref, gn_ref, win_ref, wkvt_ref, wm_ref, bm_ref, gva_ref, goa_ref, gq_ref,
                 gk_ref, gkcol_ref, a_ref, q_ref, k_ref, v_ref, zb_ref, *maybe_vn_ref,
                 mix_block, kv_feature_major):
    tm = x_ref.shape[1]
    x = x_ref[0]
    h = (x * _rsqrt_mean(x, D_MODEL) * gn_ref[...]).astype(BF16)

    def seg(i):
        return jnp.dot(h, win_ref[:, i * SEG:(i + 1) * SEG], preferred_element_type=F32)

    zb_ref[0] = seg(SEG_ZB).astype(zb_ref.dtype)
    q = seg(SEG_Q)
    for j in range(D_B // LANES):
        sl = slice(j * LANES, (j + 1) * LANES)
        qs = q[:, sl]
        q_ref[0, :, sl] = (qs * _head_rms_scale(qs) * gq_ref[:, sl] * SCALE).astype(q_ref.dtype)

    if kv_feature_major:
        kvt = lax.dot_general(wkvt_ref[...], h, NT, preferred_element_type=F32)
        v_ref[0] = kvt[D_B:]
        kt = kvt[:D_B].reshape(N_HEADS_B, HEAD_DIM, tm)
        ms = jnp.sum(kt * kt, axis=1, keepdims=True) * (1.0 / HEAD_DIM)
        kn = kt * lax.rsqrt(ms + EPS) * gkcol_ref[...].reshape(N_HEADS_B, HEAD_DIM, 1)
        k_ref[0] = kn.reshape(D_B, tm)
    else:
        v_ref[0] = seg(SEG_VB)
        k = seg(SEG_K)
        for j in range(D_B // LANES):
            sl = slice(j * LANES, (j + 1) * LANES)
            ks = k[:, sl]
            k_ref[0, :, sl] = ks * _head_rms_scale(ks) * gk_ref[:, sl]

    u, va, za = seg(SEG_U), seg(SEG_VA), seg(SEG_ZA)
    row = lax.broadcasted_iota(jnp.int32, (CHUNK, CHUNK), 0)
    col = lax.broadcasted_iota(jnp.int32, (CHUNK, CHUNK), 1)
    mix_mask = (row >= col) & ((row // mix_block) == (col // mix_block))
    for g in range(N_GROUPS_A):
        sl = slice(g * GROUP_A, (g + 1) * GROUP_A)
        vs = va[:, sl]
        vn = vs * _rsqrt_mean(vs, GROUP_A) * gva_ref[:, sl]
        if maybe_vn_ref:
            maybe_vn_ref[0][0, :, sl] = vn
        vn16 = vn.astype(BF16)
        wm = jnp.where(mix_mask, wm_ref[g], 0.0).astype(BF16)
        mixed = jnp.concatenate(
            [jnp.dot(wm, vn16[c * CHUNK:(c + 1) * CHUNK], preferred_element_type=F32) + bm_ref[g]
             for c in range(tm // CHUNK)], axis=0)
        a = u[:, sl] * mixed
        a = a * _rsqrt_mean(a, GROUP_A) * goa_ref[:, sl]
        a_ref[0, :, sl] = (a * _silu(za[:, sl])).astype(a_ref.dtype)


def _proj_call(x, gn, win16, wkvt16, wm, bm, gva, goa, gq, gk, gkcol, *, mix_block, q_dtype,
               kv_feature_major, emit_vn):
    g, r, _ = x.shape
    tm = min(TM_PROJ, r)
    row = lambda w: pl.BlockSpec((1, tm, w), lambda b, i: (b, i, 0))
    if kv_feature_major:
        kv_shape, kv_spec = (g, D_B, r), pl.BlockSpec((1, D_B, tm), lambda b, i: (b, 0, i))
    else:
        kv_shape, kv_spec = (g, r, D_B), row(D_B)
    out_shape = [jax.ShapeDtypeStruct((g, r, D_A), BF16),
                 jax.ShapeDtypeStruct((g, r, D_B), q_dtype),
                 jax.ShapeDtypeStruct(kv_shape, F32),
                 jax.ShapeDtypeStruct(kv_shape, F32),
                 jax.ShapeDtypeStruct((g, r, D_B), BF16)]
    out_specs = [row(D_A), row(D_B), kv_spec, kv_spec, row(D_B)]
    if emit_vn:
        out_shape.append(jax.ShapeDtypeStruct((g, r, D_A), F32))
        out_specs.append(row(D_A))
    return pl.pallas_call(
        functools.partial(_proj_kernel, mix_block=mix_block, kv_feature_major=kv_feature_major),
        grid=(g, r // tm),
        in_specs=[row(D_MODEL), _full((1, D_MODEL)), _full(win16.shape), _full(wkvt16.shape),
                  _full(wm.shape), _full(bm.shape), _full((1, D_A)), _full((1, D_A)),
                  _full((1, D_B)), _full((1, D_B)), _full((D_B, 1))],
        out_specs=out_specs,
        out_shape=out_shape,
        compiler_params=pltpu.CompilerParams(dimension_semantics=("parallel", "parallel"),
                                             vmem_limit_bytes=VMEM_LIMIT),
        name="proj",
    )(x, gn, win16, wkvt16, wm, bm, gva, goa, gq, gk, gkcol)


def _gated_head_norm(o, zb, gob):
    return o * _head_rms_scale(o) * gob * _silu(zb)


def _prompt_attn_kernel(q_ref, kt_ref, vt_ref, bias_ref, zb_ref, gob_ref, b_ref, k16, v16):
    i = pl.program_id(1)

    @pl.when(i == 0)
    def _():
        for t in range(k16.shape[0]):
            cols = slice(t * TQ, (t + 1) * TQ)
            k16[t] = kt_ref[0, :, cols].astype(BF16)
            v16[t] = vt_ref[0, :, cols].astype(BF16)

    lane_lo = lax.broadcasted_iota(jnp.int32, (TQ, LANES), 1) < HEAD_DIM
    for p in range(D_B // LANES):
        sl = slice(p * LANES, (p + 1) * LANES)
        q2 = q_ref[0, :, sl]
        q_heads = (jnp.where(lane_lo, q2, jnp.zeros_like(q2)),
                   jnp.where(lane_lo, jnp.zeros_like(q2), q2))

        def body(t, carry):
            k2 = k16[t, sl, :]
            v2 = v16[t, sl, :]
            bias = bias_ref[jnp.minimum(i - t, bias_ref.shape[0] - 1)]
            new = []
            for qh, (m, l, acc) in zip(q_heads, carry):
                s = jnp.dot(qh, k2, preferred_element_type=F32) + bias
                m_new = jnp.maximum(m, jnp.max(s, axis=-1, keepdims=True))
                alpha = jnp.exp(m - m_new)
                pr = jnp.exp(s - m_new)
                l = alpha * l + jnp.sum(pr, axis=-1, keepdims=True)
                acc = alpha * acc + lax.dot_general(pr.astype(BF16), v2, NT,
                                                    preferred_element_type=F32)
                new.append((m_new, l, acc))
            return tuple(new)

        init = (jnp.full((TQ, 1), NEG, F32), jnp.zeros((TQ, 1), F32),
                jnp.zeros((TQ, LANES), F32))
        (_, l_e, acc_e), (_, l_o, acc_o) = lax.fori_loop(0, i + 1, body, (init, init))
        o = jnp.where(lane_lo, acc_e / l_e, acc_o / l_o)
        b_ref[0, :, sl] = _gated_head_norm(
            o, zb_ref[0, :, sl].astype(F32), gob_ref[:, sl]).astype(b_ref.dtype)


def _prompt_attn_call(q, kt, vt, zb, gob):
    bsz, s, _ = q.shape
    near = max(w for w, _ in PATTERNS[:-1])
    nkd = (near + TQ - 1) // TQ + 2
    rc = np.arange(TQ)[:, None] - np.arange(TQ)[None, :]
    bias = jnp.asarray(np.stack([_log_bias(kd * TQ + rc) for kd in range(nkd)]))
    tile = pl.BlockSpec((1, TQ, D_B), lambda b, i: (b, i, 0))
    whole = pl.BlockSpec((1, D_B, s), lambda b, i: (b, 0, 0))
    return pl.pallas_call(
        _prompt_attn_kernel,
        grid=(bsz, s // TQ),
        in_specs=[tile, whole, whole, _full(bias.shape), tile, _full((1, D_B))],
        out_specs=tile,
        out_shape=jax.ShapeDtypeStruct((bsz, s, D_B), BF16),
        scratch_shapes=[pltpu.VMEM((s // TQ, D_B, TQ), BF16)] * 2,
        compiler_params=pltpu.CompilerParams(dimension_semantics=("parallel", "arbitrary"),
                                             vmem_limit_bytes=VMEM_LIMIT),
        name="prompt_attn",
    )(q, kt, vt, bias, zb, gob)


NEW_PAD = LANES


def _sample_attn_kernel(q_ref, kn_ref, vn_ref, ckt_ref, cvt_ref, bc_ref, bn_ref, zb_ref, gob_ref,
                        b_ref):
    t = q_ref.shape[1]
    rows = N_HEADS_B * t
    r_i = lax.broadcasted_iota(jnp.int32, (rows, D_B), 0)
    c_i = lax.broadcasted_iota(jnp.int32, (rows, D_B), 1)
    head_mask = (r_i // t) == (c_i // HEAD_DIM)
    q_rep = jnp.concatenate([q_ref[0]] * N_HEADS_B, axis=0)
    q_bd = jnp.where(head_mask, q_rep, 0.0).astype(BF16)

    pad = jnp.zeros((NEW_PAD - t, D_B), F32)
    k_new = jnp.concatenate([kn_ref[0], pad], axis=0).astype(BF16)
    v_new = jnp.concatenate([vn_ref[0], pad], axis=0).astype(BF16)

    s_c = jnp.dot(q_bd, ckt_ref[0].astype(BF16), preferred_element_type=F32) + bc_ref[...]
    s_n = lax.dot_general(q_bd, k_new, NT, preferred_element_type=F32) + bn_ref[...]
    m = jnp.maximum(jnp.max(s_c, axis=-1, keepdims=True), jnp.max(s_n, axis=-1, keepdims=True))
    p_c = jnp.exp(s_c - m)
    p_n = jnp.exp(s_n - m)
    l = jnp.sum(p_c, axis=-1, keepdims=True) + jnp.sum(p_n, axis=-1, keepdims=True)
    o = (lax.dot_general(p_c.astype(BF16), cvt_ref[0].astype(BF16), NT,
                         preferred_element_type=F32)
         + jnp.dot(p_n.astype(BF16), v_new, preferred_element_type=F32)) / l
    o = jnp.where(head_mask, o, 0.0)
    o_tok = o[0:t]
    for hd in range(1, N_HEADS_B):
        o_tok = o_tok + o[hd * t:(hd + 1) * t]
    for p in range(D_B // LANES):
        sl = slice(p * LANES, (p + 1) * LANES)
        b_ref[0, :, sl] = _gated_head_norm(
            o_tok[:, sl], zb_ref[0, :, sl].astype(F32), gob_ref[:, sl]).astype(b_ref.dtype)


def _sample_attn_call(q, k_new, v_new, cache_kt, cache_vt, zb, gob):
    bsz, t, _ = q.shape
    wb = cache_kt.shape[2]
    tq = np.tile(np.arange(t), N_HEADS_B)[:, None]
    bias_c = jnp.asarray(_log_bias(wb + tq - np.arange(wb)[None, :]))
    d_new = np.where(np.arange(NEW_PAD)[None, :] < t, tq - np.arange(NEW_PAD)[None, :], -1)
    bias_n = jnp.asarray(_log_bias(d_new))
    tok = pl.BlockSpec((1, t, D_B), lambda b: (b, 0, 0))
    win = pl.BlockSpec((1, D_B, wb), lambda b: (b, 0, 0))
    return pl.pallas_call(
        _sample_attn_kernel,
        grid=(bsz,),
        in_specs=[tok, tok, tok, win, win, _full(bias_c.shape), _full(bias_n.shape), tok,
                  _full((1, D_B))],
        out_specs=tok,
        out_shape=jax.ShapeDtypeStruct((bsz, t, D_B), F32),
        compiler_params=pltpu.CompilerParams(dimension_semantics=("parallel",),
                                             vmem_limit_bytes=VMEM_LIMIT),
        name="sample_attn",
    )(q, k_new, v_new, cache_kt, cache_vt, bias_c, bias_n, zb, gob)


def _finish_kernel(x_ref, a_ref, b_ref, p_ref, wout_ref, wple_ref, wgate_ref, gple_ref, y_ref):
    h = (x_ref[...]
         + jnp.dot(a_ref[...].astype(BF16), wout_ref[:D_A, :], preferred_element_type=F32)
         + jnp.dot(b_ref[...].astype(BF16), wout_ref[D_A:, :], preferred_element_type=F32))
    gate = jax.nn.sigmoid(jnp.dot(h.astype(BF16), wgate_ref[...], preferred_element_type=F32))
    e = jnp.dot(p_ref[...].astype(BF16), wple_ref[...], preferred_element_type=F32)
    e = e * _rsqrt_mean(e, D_MODEL) * gple_ref[...]
    y_ref[...] = h + gate * e


def _finish_call(x, a, b, p, wout16, wple16, wgate16, gple):
    n = x.shape[0]
    tm = min(TM_FIN, n)
    row = lambda w: pl.BlockSpec((tm, w), lambda i: (i, 0))
    return pl.pallas_call(
        _finish_kernel,
        grid=(n // tm,),
        in_specs=[row(D_MODEL), row(D_A), row(D_B), row(D_PLE), _full(wout16.shape),
                  _full(wple16.shape), _full(wgate16.shape), _full((1, D_MODEL))],
        out_specs=row(D_MODEL),
        out_shape=jax.ShapeDtypeStruct((n, D_MODEL), F32),
        compiler_params=pltpu.CompilerParams(dimension_semantics=("parallel",),
                                             vmem_limit_bytes=VMEM_LIMIT),
        name="finish",
    )(x, a, b, p, wout16, wple16, wgate16, gple)


def _feature_major(c):
    b, p, h, dh = c.shape
    return jnp.transpose(c, (0, 2, 3, 1)).reshape(b, h * dh, p)


def _position_major(ct, h):
    b, hd, p = ct.shape
    return jnp.transpose(ct.reshape(b, h, hd // h, p), (0, 3, 1, 2))


def kernel(x_prompt, x_sample, cache_k, cache_v, p_prompt, p_sample, g_norm, w_in, w_s, b_s,
           g_va, g_oa, g_q, g_k, g_ob, w_out, w_ple, g_ple, w_ple_gate):
    depth = w_in.shape[0]
    assert depth == 1, "single-layer step"
    bsz, seq, _ = x_prompt.shape
    dbs, dseq, _ = x_sample.shape
    assert seq % TQ == 0 and seq <= max(w for w, _ in PATTERNS) and CHUNK % dseq == 0
    i = 0

    gn = g_norm[i].reshape(1, D_MODEL)
    win16 = w_in[i].astype(BF16)
    wkvt16 = jnp.transpose(w_in[i][:, SEG_K * SEG:(SEG_VB + 1) * SEG]).astype(BF16)
    gva = g_va[i].reshape(1, D_A)
    goa = g_oa[i].reshape(1, D_A)
    gq = jnp.tile(g_q[i], N_HEADS_B).reshape(1, D_B)
    gk = jnp.tile(g_k[i], N_HEADS_B).reshape(1, D_B)
    gkcol = gk.reshape(D_B, 1)
    gob = g_ob[i].reshape(1, D_B)
    wout16 = w_out[i].astype(BF16)
    wple16 = w_ple[i].astype(BF16)
    wgate16 = w_ple_gate[i].astype(BF16)
    gple = g_ple[i].reshape(1, D_MODEL)

    a_p, q_p, kt_p, vt_p, zb_p = _proj_call(
        x_prompt, gn, win16, wkvt16, w_s[i], b_s[i][:, :, None], gva, goa, gq, gk, gkcol,
        mix_block=CHUNK, q_dtype=BF16, kv_feature_major=True, emit_vn=False)
    b_p = _prompt_attn_call(q_p, kt_p, vt_p, zb_p, gob)
    y_p = _finish_call(x_prompt.reshape(bsz * seq, D_MODEL), a_p.reshape(bsz * seq, D_A),
                       b_p.reshape(bsz * seq, D_B), p_prompt[i].reshape(bsz * seq, D_PLE),
                       wout16, wple16, wgate16, gple)

    rep = CHUNK // dseq
    wm_s = jnp.tile(w_s[i][:, :dseq, :dseq], (1, rep, rep))
    bm_s = jnp.tile(b_s[i][:, :dseq], (1, rep))[:, :, None]
    xs = x_sample.reshape(1, dbs * dseq, D_MODEL)
    a_s, q_s, k_s, v_s, zb_s, vn_s = _proj_call(
        xs, gn, win16, wkvt16, wm_s, bm_s, gva, goa, gq, gk, gkcol,
        mix_block=dseq, q_dtype=F32, kv_feature_major=False, emit_vn=True)
    tok = lambda z: z.reshape(dbs, dseq, D_B)
    b_s_ = _sample_attn_call(tok(q_s), tok(k_s), tok(v_s), _feature_major(cache_k[i]),
                             _feature_major(cache_v[i]), tok(zb_s), gob)
    y_s = _finish_call(xs.reshape(dbs * dseq, D_MODEL), a_s.reshape(dbs * dseq, D_A),
                       b_s_.reshape(dbs * dseq, D_B), p_sample[i].reshape(dbs * dseq, D_PLE),
                       wout16, wple16, wgate16, gple)

    hshape = (N_HEADS_B, HEAD_DIM)
    return (y_p.reshape(bsz, seq, D_MODEL),
            y_s.reshape(dbs, dseq, D_MODEL),
            _position_major(kt_p, N_HEADS_B)[None],
            _position_major(vt_p, N_HEADS_B)[None],
            k_s.reshape(1, dbs, dseq, *hshape),
            v_s.reshape(1, dbs, dseq, *hshape),
            vn_s.reshape(1, dbs, dseq, D_A))
```

```python
import functools

import numpy as np
import jax
import jax.numpy as jnp
from jax import lax
from jax.experimental import pallas as pl
from jax.experimental.pallas import tpu as pltpu

D_MODEL = 1024
D_A = 512
N_GROUPS_A = 4
GROUP_A = 128
CHUNK = 128
D_B = 512
N_HEADS_B = 8
HEAD_DIM = 64
PATTERNS = ((128, 1), (512, 4), (2048, 16))
D_PLE = 256
EPS = 1e-6
NEG = -1e30
SCALE = HEAD_DIM ** -0.5
SEG = 512
SEG_U, SEG_VA, SEG_ZA, SEG_Q, SEG_K, SEG_VB, SEG_ZB = range(7)

LANES = 128
VMEM_LIMIT = 48 * 1024 * 1024

TM_PROJ = 512
TM_FIN = 512
TQ = 256

BF16 = jnp.bfloat16
F32 = jnp.float32
NT = (((1,), (1,)), ((), ()))


def _multiplicity(d):
    c = np.zeros(d.shape, np.int32)
    for w, r in PATTERNS:
        c += ((d >= 0) & (d <= w) & (d % r == 0)).astype(np.int32)
    return c


def _log_bias(d):
    c = _multiplicity(d)
    return np.where(c > 0, np.log(np.maximum(c, 1).astype(np.float64)), NEG).astype(np.float32)


def _rsqrt_mean(x, n):
    return lax.rsqrt(jnp.sum(x * x, axis=-1, keepdims=True) * (1.0 / n) + EPS)


def _head_rms_scale(x):
    lane_lo = lax.broadcasted_iota(jnp.int32, x.shape, 1) < HEAD_DIM
    x2 = x * x
    lo = jnp.sum(jnp.where(lane_lo, x2, 0.0), axis=-1, keepdims=True)
    hi = jnp.sum(jnp.where(lane_lo, 0.0, x2), axis=-1, keepdims=True)
    ms = jnp.where(lane_lo, lo, hi) * (1.0 / HEAD_DIM)
    return lax.rsqrt(ms + EPS)


def _silu(z):
    return z * jax.nn.sigmoid(z)


def _full(shape):
    return pl.BlockSpec(shape, lambda *_: (0,) * len(shape))


def _head_norm_feature_major(xt, gcol_ref):
    m = xt.shape[1]
    x3 = xt.reshape(N_HEADS_B, HEAD_DIM, m)
    ms = jnp.sum(x3 * x3, axis=1, keepdims=True) * (1.0 / HEAD_DIM)
    g3 = gcol_ref[...].reshape(N_HEADS_B, HEAD_DIM, 1)
    return (x3 * lax.rsqrt(ms + EPS) * g3).reshape(D_B, m)


def _proj_kernel(x_ref, gn_ref, win_ref, wqkvt_ref, wm_ref, bm_ref, gva_ref, goa_ref, gq_ref,
                 gk_ref, gqcol_ref, gkcol_ref, a_ref, q_ref, k_ref, v_ref, zb_ref, *maybe_vn_ref,
                 mix_block, feature_major):
    tm = x_ref.shape[1]
    x = x_ref[0]
    h = (x * _rsqrt_mean(x, D_MODEL) * gn_ref[...]).astype(BF16)

    def seg(i):
        return jnp.dot(h, win_ref[:, i * SEG:(i + 1) * SEG], preferred_element_type=F32)

    zb_ref[0] = seg(SEG_ZB).astype(zb_ref.dtype)
    if feature_major:
        qkvt = lax.dot_general(wqkvt_ref[...], h, NT, preferred_element_type=F32)
        q_ref[0] = (_head_norm_feature_major(qkvt[:D_B], gqcol_ref) * SCALE).astype(q_ref.dtype)
        k_ref[0] = _head_norm_feature_major(qkvt[D_B:2 * D_B], gkcol_ref)
        v_ref[0] = qkvt[2 * D_B:]
    else:
        q, k = seg(SEG_Q), seg(SEG_K)
        v_ref[0] = seg(SEG_VB)
        for j in range(D_B // LANES):
            sl = slice(j * LANES, (j + 1) * LANES)
            qs, ks = q[:, sl], k[:, sl]
            q_ref[0, :, sl] = (qs * _head_rms_scale(qs) * gq_ref[:, sl] * SCALE).astype(q_ref.dtype)
            k_ref[0, :, sl] = ks * _head_rms_scale(ks) * gk_ref[:, sl]

    u, va, za = seg(SEG_U), seg(SEG_VA), seg(SEG_ZA)
    row = lax.broadcasted_iota(jnp.int32, (CHUNK, CHUNK), 0)
    col = lax.broadcasted_iota(jnp.int32, (CHUNK, CHUNK), 1)
    mix_mask = (row >= col) & ((row // mix_block) == (col // mix_block))
    for g in range(N_GROUPS_A):
        sl = slice(g * GROUP_A, (g + 1) * GROUP_A)
        vs = va[:, sl]
        vn = vs * _rsqrt_mean(vs, GROUP_A) * gva_ref[:, sl]
        if maybe_vn_ref:
            maybe_vn_ref[0][0, :, sl] = vn
        vn16 = vn.astype(BF16)
        wm = jnp.where(mix_mask, wm_ref[g], 0.0).astype(BF16)
        mixed = jnp.concatenate(
            [jnp.dot(wm, vn16[c * CHUNK:(c + 1) * CHUNK], preferred_element_type=F32) + bm_ref[g]
             for c in range(tm // CHUNK)], axis=0)
        a = u[:, sl] * mixed
        a = a * _rsqrt_mean(a, GROUP_A) * goa_ref[:, sl]
        a_ref[0, :, sl] = (a * _silu(za[:, sl])).astype(a_ref.dtype)


def _proj_call(x, gn, win16, wqkvt16, wm, bm, gva, goa, gq, gk, *, mix_block, q_dtype,
               feature_major, emit_vn):
    g, r, _ = x.shape
    tm = min(TM_PROJ, r)
    row = lambda w: pl.BlockSpec((1, tm, w), lambda b, i: (b, i, 0))
    if feature_major:
        qkv_shape, qkv_spec = (g, D_B, r), pl.BlockSpec((1, D_B, tm), lambda b, i: (b, 0, i))
    else:
        qkv_shape, qkv_spec = (g, r, D_B), row(D_B)
    out_shape = [jax.ShapeDtypeStruct((g, r, D_A), BF16),
                 jax.ShapeDtypeStruct(qkv_shape, q_dtype),
                 jax.ShapeDtypeStruct(qkv_shape, F32),
                 jax.ShapeDtypeStruct(qkv_shape, F32),
                 jax.ShapeDtypeStruct((g, r, D_B), BF16)]
    out_specs = [row(D_A), qkv_spec, qkv_spec, qkv_spec, row(D_B)]
    if emit_vn:
        out_shape.append(jax.ShapeDtypeStruct((g, r, D_A), F32))
        out_specs.append(row(D_A))
    return pl.pallas_call(
        functools.partial(_proj_kernel, mix_block=mix_block, feature_major=feature_major),
        grid=(g, r // tm),
        in_specs=[row(D_MODEL), _full((1, D_MODEL)), _full(win16.shape), _full(wqkvt16.shape),
                  _full(wm.shape), _full(bm.shape), _full((1, D_A)), _full((1, D_A)),
                  _full((1, D_B)), _full((1, D_B)), _full((D_B, 1)), _full((D_B, 1))],
        out_specs=out_specs,
        out_shape=out_shape,
        compiler_params=pltpu.CompilerParams(dimension_semantics=("parallel", "parallel"),
                                             vmem_limit_bytes=VMEM_LIMIT),
        name="proj",
    )(x, gn, win16, wqkvt16, wm, bm, gva, goa, gq, gk, gq.reshape(D_B, 1), gk.reshape(D_B, 1))


def _gated_head_norm(o, zb, gob):
    return o * _head_rms_scale(o) * gob * _silu(zb)


V_ROWS = HEAD_DIM + 16


def _prompt_attn_kernel(qt_ref, kt_ref, vt_ref, bias_ref, zb_ref, gobcol_ref, b_ref,
                        k16, v1, qm, m_sc, acc_sc):
    i = pl.program_id(1)

    @pl.when(i == 0)
    def _():
        ones = jnp.ones((V_ROWS - HEAD_DIM, TQ), BF16)
        for t in range(k16.shape[0]):
            cols = slice(t * TQ, (t + 1) * TQ)
            k16[t] = kt_ref[0, :, cols].T.astype(BF16)
            vt = vt_ref[0, :, cols].astype(BF16)
            for h in range(N_HEADS_B):
                v1[t, h, :HEAD_DIM, :] = vt[h * HEAD_DIM:(h + 1) * HEAD_DIM]
                v1[t, h, HEAD_DIM:, :] = ones

    row_lo = lax.broadcasted_iota(jnp.int32, (LANES, TQ), 0) < HEAD_DIM
    for p in range(D_B // LANES):
        q2 = qt_ref[0, p * LANES:(p + 1) * LANES, :]
        qm[2 * p] = jnp.where(row_lo, q2, jnp.zeros_like(q2))
        qm[2 * p + 1] = jnp.where(row_lo, jnp.zeros_like(q2), q2)
    m_sc[...] = jnp.full(m_sc.shape, NEG, F32)
    acc_sc[...] = jnp.zeros(acc_sc.shape, F32)

    def body(t, carry):
        bias = bias_ref[jnp.minimum(i - t, bias_ref.shape[0] - 1)]

        def scores(h):
            p = h // 2
            return jnp.dot(k16[t, :, p * LANES:(p + 1) * LANES], qm[h],
                           preferred_element_type=F32)

        sts = [scores(h) + bias for h in range(N_HEADS_B)]
        tops = [jnp.max(st, axis=0, keepdims=True) for st in sts]
        floor = functools.reduce(jnp.minimum, tops, jnp.full_like(tops[0], NEG))
        for h in range(N_HEADS_B):
            m_old = m_sc[h]
            m_new = jnp.maximum(jnp.maximum(m_old, tops[h]), floor)
            alpha = jnp.exp(m_old - m_new)
            pt = jnp.exp(sts[h] - m_new).astype(BF16)
            acc_sc[h] = alpha * acc_sc[h] + jnp.dot(v1[t, h], pt, preferred_element_type=F32)
            m_sc[h] = m_new
        return carry

    lax.fori_loop(0, i + 1, body, 0)

    for p in range(D_B // LANES):
        halves = []
        for h in (2 * p, 2 * p + 1):
            acc = acc_sc[h]
            o = acc[:HEAD_DIM] / acc[HEAD_DIM:HEAD_DIM + 1]
            ms = jnp.sum(o * o, axis=0, keepdims=True) * (1.0 / HEAD_DIM)
            halves.append(o * lax.rsqrt(ms + EPS) * gobcol_ref[h * HEAD_DIM:(h + 1) * HEAD_DIM])
        sl = slice(p * LANES, (p + 1) * LANES)
        pair = jnp.concatenate(halves, axis=0).T
        b_ref[0, :, sl] = (pair * _silu(zb_ref[0, :, sl].astype(F32))).astype(b_ref.dtype)


def _prompt_attn_call(qt, kt, vt, zb, gob):
    bsz, _, s = qt.shape
    near = max(w for w, _ in PATTERNS[:-1])
    nkd = (near + TQ - 1) // TQ + 2
    qk = np.arange(TQ)[None, :] - np.arange(TQ)[:, None]
    bias = jnp.asarray(np.stack([_log_bias(kd * TQ + qk) for kd in range(nkd)]))
    nt = s // TQ
    qtile = pl.BlockSpec((1, D_B, TQ), lambda b, i: (b, 0, i))
    tile = pl.BlockSpec((1, TQ, D_B), lambda b, i: (b, i, 0))
    whole = pl.BlockSpec((1, D_B, s), lambda b, i: (b, 0, 0))
    return pl.pallas_call(
        _prompt_attn_kernel,
        grid=(bsz, nt),
        in_specs=[qtile, whole, whole, _full(bias.shape), tile, _full((D_B, 1))],
        out_specs=tile,
        out_shape=jax.ShapeDtypeStruct((bsz, s, D_B), BF16),
        scratch_shapes=[pltpu.VMEM((nt, TQ, D_B), BF16),
                        pltpu.VMEM((nt, N_HEADS_B, V_ROWS, TQ), BF16),
                        pltpu.VMEM((N_HEADS_B, LANES, TQ), BF16),
                        pltpu.VMEM((N_HEADS_B, 1, TQ), F32),
                        pltpu.VMEM((N_HEADS_B, V_ROWS, TQ), F32)],
        compiler_params=pltpu.CompilerParams(dimension_semantics=("parallel", "arbitrary"),
                                             vmem_limit_bytes=VMEM_LIMIT),
        name="prompt_attn",
    )(qt, kt, vt, bias, zb, gob.reshape(D_B, 1))


NEW_PAD = LANES


def _sample_attn_kernel(q_ref, kn_ref, vn_ref, ckt_ref, cvt_ref, bc_ref, bn_ref, zb_ref, gob_ref,
                        b_ref):
    t = q_ref.shape[1]
    rows = N_HEADS_B * t
    r_i = lax.broadcasted_iota(jnp.int32, (rows, D_B), 0)
    c_i = lax.broadcasted_iota(jnp.int32, (rows, D_B), 1)
    head_mask = (r_i // t) == (c_i // HEAD_DIM)
    q_rep = jnp.concatenate([q_ref[0]] * N_HEADS_B, axis=0)
    q_bd = jnp.where(head_mask, q_rep, 0.0).astype(BF16)

    pad = jnp.zeros((NEW_PAD - t, D_B), F32)
    k_new = jnp.concatenate([kn_ref[0], pad], axis=0).astype(BF16)
    v_new = jnp.concatenate([vn_ref[0], pad], axis=0).astype(BF16)

    s_c = jnp.dot(q_bd, ckt_ref[0].astype(BF16), preferred_element_type=F32) + bc_ref[...]
    s_n = lax.dot_general(q_bd, k_new, NT, preferred_element_type=F32) + bn_ref[...]
    m = jnp.maximum(jnp.max(s_c, axis=-1, keepdims=True), jnp.max(s_n, axis=-1, keepdims=True))
    p_c = jnp.exp(s_c - m)
    p_n = jnp.exp(s_n - m)
    l = jnp.sum(p_c, axis=-1, keepdims=True) + jnp.sum(p_n, axis=-1, keepdims=True)
    o = (lax.dot_general(p_c.astype(BF16), cvt_ref[0].astype(BF16), NT,
                         preferred_element_type=F32)
         + jnp.dot(p_n.astype(BF16), v_new, preferred_element_type=F32)) / l
    o = jnp.where(head_mask, o, 0.0)
    o_tok = o[0:t]
    for hd in range(1, N_HEADS_B):
        o_tok = o_tok + o[hd * t:(hd + 1) * t]
    for p in range(D_B // LANES):
        sl = slice(p * LANES, (p + 1) * LANES)
        b_ref[0, :, sl] = _gated_head_norm(
            o_tok[:, sl], zb_ref[0, :, sl].astype(F32), gob_ref[:, sl]).astype(b_ref.dtype)


def _sample_attn_call(q, k_new, v_new, cache_kt, cache_vt, zb, gob):
    bsz, t, _ = q.shape
    wb = cache_kt.shape[2]
    tq = np.tile(np.arange(t), N_HEADS_B)[:, None]
    bias_c = jnp.asarray(_log_bias(wb + tq - np.arange(wb)[None, :]))
    d_new = np.where(np.arange(NEW_PAD)[None, :] < t, tq - np.arange(NEW_PAD)[None, :], -1)
    bias_n = jnp.asarray(_log_bias(d_new))
    tok = pl.BlockSpec((1, t, D_B), lambda b: (b, 0, 0))
    win = pl.BlockSpec((1, D_B, wb), lambda b: (b, 0, 0))
    return pl.pallas_call(
        _sample_attn_kernel,
        grid=(bsz,),
        in_specs=[tok, tok, tok, win, win, _full(bias_c.shape), _full(bias_n.shape), tok,
                  _full((1, D_B))],
        out_specs=tok,
        out_shape=jax.ShapeDtypeStruct((bsz, t, D_B), F32),
        compiler_params=pltpu.CompilerParams(dimension_semantics=("parallel",),
                                             vmem_limit_bytes=VMEM_LIMIT),
        name="sample_attn",
    )(q, k_new, v_new, cache_kt, cache_vt, bias_c, bias_n, zb, gob)


def _finish_kernel(x_ref, a_ref, b_ref, p_ref, wout_ref, wple_ref, wgate_ref, gple_ref, y_ref):
    h = (x_ref[...]
         + jnp.dot(a_ref[...].astype(BF16), wout_ref[:D_A, :], preferred_element_type=F32)
         + jnp.dot(b_ref[...].astype(BF16), wout_ref[D_A:, :], preferred_element_type=F32))
    gate = jax.nn.sigmoid(jnp.dot(h.astype(BF16), wgate_ref[...], preferred_element_type=F32))
    e = jnp.dot(p_ref[...].astype(BF16), wple_ref[...], preferred_element_type=F32)
    e = e * _rsqrt_mean(e, D_MODEL) * gple_ref[...]
    y_ref[...] = h + gate * e


def _finish_call(x, a, b, p, wout16, wple16, wgate16, gple):
    n = x.shape[0]
    tm = min(TM_FIN, n)
    row = lambda w: pl.BlockSpec((tm, w), lambda i: (i, 0))
    return pl.pallas_call(
        _finish_kernel,
        grid=(n // tm,),
        in_specs=[row(D_MODEL), row(D_A), row(D_B), row(D_PLE), _full(wout16.shape),
                  _full(wple16.shape), _full(wgate16.shape), _full((1, D_MODEL))],
        out_specs=row(D_MODEL),
        out_shape=jax.ShapeDtypeStruct((n, D_MODEL), F32),
        compiler_params=pltpu.CompilerParams(dimension_semantics=("parallel",),
                                             vmem_limit_bytes=VMEM_LIMIT),
        name="finish",
    )(x, a, b, p, wout16, wple16, wgate16, gple)


def _feature_major(c):
    b, p, h, dh = c.shape
    return jnp.transpose(c, (0, 2, 3, 1)).reshape(b, h * dh, p)


def _position_major(ct, h):
    b, hd, p = ct.shape
    return jnp.transpose(ct.reshape(b, h, hd // h, p), (0, 3, 1, 2))


def kernel(x_prompt, x_sample, cache_k, cache_v, p_prompt, p_sample, g_norm, w_in, w_s, b_s,
           g_va, g_oa, g_q, g_k, g_ob, w_out, w_ple, g_ple, w_ple_gate):
    depth = w_in.shape[0]
    assert depth == 1, "single-layer step"
    bsz, seq, _ = x_prompt.shape
    dbs, dseq, _ = x_sample.shape
    assert seq % TQ == 0 and seq <= max(w for w, _ in PATTERNS) and CHUNK % dseq == 0
    i = 0

    gn = g_norm[i].reshape(1, D_MODEL)
    win16 = w_in[i].astype(BF16)
    wqkvt16 = jnp.transpose(w_in[i][:, SEG_Q * SEG:(SEG_VB + 1) * SEG]).astype(BF16)
    gva = g_va[i].reshape(1, D_A)
    goa = g_oa[i].reshape(1, D_A)
    gq = jnp.tile(g_q[i], N_HEADS_B).reshape(1, D_B)
    gk = jnp.tile(g_k[i], N_HEADS_B).reshape(1, D_B)
    gob = g_ob[i].reshape(1, D_B)
    wout16 = w_out[i].astype(BF16)
    wple16 = w_ple[i].astype(BF16)
    wgate16 = w_ple_gate[i].astype(BF16)
    gple = g_ple[i].reshape(1, D_MODEL)

    a_p, qt_p, kt_p, vt_p, zb_p = _proj_call(
        x_prompt, gn, win16, wqkvt16, w_s[i], b_s[i][:, :, None], gva, goa, gq, gk,
        mix_block=CHUNK, q_dtype=BF16, feature_major=True, emit_vn=False)
    b_p = _prompt_attn_call(qt_p, kt_p, vt_p, zb_p, gob)
    y_p = _finish_call(x_prompt.reshape(bsz * seq, D_MODEL), a_p.reshape(bsz * seq, D_A),
                       b_p.reshape(bsz * seq, D_B), p_prompt[i].reshape(bsz * seq, D_PLE),
                       wout16, wple16, wgate16, gple)

    rep = CHUNK // dseq
    wm_s = jnp.tile(w_s[i][:, :dseq, :dseq], (1, rep, rep))
    bm_s = jnp.tile(b_s[i][:, :dseq], (1, rep))[:, :, None]
    xs = x_sample.reshape(1, dbs * dseq, D_MODEL)
    a_s, q_s, k_s, v_s, zb_s, vn_s = _proj_call(
        xs, gn, win16, wqkvt16, wm_s, bm_s, gva, goa, gq, gk,
        mix_block=dseq, q_dtype=F32, feature_major=False, emit_vn=True)
    tok = lambda z: z.reshape(dbs, dseq, D_B)
    b_s_ = _sample_attn_call(tok(q_s), tok(k_s), tok(v_s), _feature_major(cache_k[i]),
                             _feature_major(cache_v[i]), tok(zb_s), gob)
    y_s = _finish_call(xs.reshape(dbs * dseq, D_MODEL), a_s.reshape(dbs * dseq, D_A),
                       b_s_.reshape(dbs * dseq, D_B), p_sample[i].reshape(dbs * dseq, D_PLE),
                       wout16, wple16, wgate16, gple)

    hshape = (N_HEADS_B, HEAD_DIM)
    return (y_p.reshape(bsz, seq, D_MODEL),
            y_s.reshape(dbs, dseq, D_MODEL),
            _position_major(kt_p, N_HEADS_B)[None],
            _position_major(vt_p, N_HEADS_B)[None],
            k_s.reshape(1, dbs, dseq, *hshape),
            v_s.reshape(1, dbs, dseq, *hshape),
            vn_s.reshape(1, dbs, dseq, D_A))
```

```python
import functools

import numpy as np
import jax
import jax.numpy as jnp
from jax import lax
from jax.experimental import pallas as pl
from jax.experimental.pallas import tpu as pltpu

D_MODEL = 1024
D_A = 512
N_GROUPS_A = 4
GROUP_A = 128
CHUNK = 128
D_B = 512
N_HEADS_B = 8
HEAD_DIM = 64
PATTERNS = ((128, 1), (512, 4), (2048, 16))
D_PLE = 256
EPS = 1e-6
NEG = -1e30
SCALE = HEAD_DIM ** -0.5
Q_SCALE = SCALE * float(np.log2(np.e))
SEG = 512
SEG_U, SEG_VA, SEG_ZA, SEG_Q, SEG_K, SEG_VB, SEG_ZB = range(7)

LANES = 128
VMEM_LIMIT = 48 * 1024 * 1024
VMEM_LIMIT_ATTN = 56 * 1024 * 1024

TM_PROJ = 512
TM_FIN = 512
TQ = 256

BF16 = jnp.bfloat16
F32 = jnp.float32
NT = (((1,), (1,)), ((), ()))


def _multiplicity(d):
    c = np.zeros(d.shape, np.int32)
    for w, r in PATTERNS:
        c += ((d >= 0) & (d <= w) & (d % r == 0)).astype(np.int32)
    return c


def _log2_bias(d):
    c = _multiplicity(d)
    return np.where(c > 0, np.log2(np.maximum(c, 1).astype(np.float64)), NEG).astype(np.float32)


def _rsqrt_mean(x, n):
    return lax.rsqrt(jnp.sum(x * x, axis=-1, keepdims=True) * (1.0 / n) + EPS)


def _head_rms_scale(x):
    lane_lo = lax.broadcasted_iota(jnp.int32, x.shape, 1) < HEAD_DIM
    x2 = x * x
    lo = jnp.sum(jnp.where(lane_lo, x2, 0.0), axis=-1, keepdims=True)
    hi = jnp.sum(jnp.where(lane_lo, 0.0, x2), axis=-1, keepdims=True)
    ms = jnp.where(lane_lo, lo, hi) * (1.0 / HEAD_DIM)
    return lax.rsqrt(ms + EPS)


def _silu(z):
    return z * jax.nn.sigmoid(z)


def _full(shape):
    return pl.BlockSpec(shape, lambda *_: (0,) * len(shape))


def _head_norm_feature_major(xt, gcol_ref):
    m = xt.shape[1]
    x3 = xt.reshape(N_HEADS_B, HEAD_DIM, m)
    ms = jnp.sum(x3 * x3, axis=1, keepdims=True) * (1.0 / HEAD_DIM)
    g3 = gcol_ref[...].reshape(N_HEADS_B, HEAD_DIM, 1)
    return (x3 * lax.rsqrt(ms + EPS) * g3).reshape(D_B, m)


def _proj_kernel(x_ref, gn_ref, win_ref, wqkvt_ref, wm_ref, bm_ref, gva_ref, goa_ref, gq_ref,
                 gk_ref, gqcol_ref, gkcol_ref, a_ref, q_ref, k_ref, v_ref, zb_ref, *maybe_vn_ref,
                 mix_block, feature_major):
    tm = x_ref.shape[1]
    x = x_ref[0]
    h = (x * _rsqrt_mean(x, D_MODEL) * gn_ref[...]).astype(BF16)

    def seg(i):
        return jnp.dot(h, win_ref[:, i * SEG:(i + 1) * SEG], preferred_element_type=F32)

    zb_ref[0] = seg(SEG_ZB).astype(zb_ref.dtype)
    if feature_major:
        qkvt = lax.dot_general(wqkvt_ref[...], h, NT, preferred_element_type=F32)
        q_ref[0] = (_head_norm_feature_major(qkvt[:D_B], gqcol_ref) * Q_SCALE).astype(q_ref.dtype)
        k_ref[0] = _head_norm_feature_major(qkvt[D_B:2 * D_B], gkcol_ref)
        v_ref[0] = qkvt[2 * D_B:]
    else:
        q, k = seg(SEG_Q), seg(SEG_K)
        v_ref[0] = seg(SEG_VB)
        for j in range(D_B // LANES):
            sl = slice(j * LANES, (j + 1) * LANES)
            qs, ks = q[:, sl], k[:, sl]
            q_ref[0, :, sl] = (qs * _head_rms_scale(qs) * gq_ref[:, sl] * Q_SCALE).astype(q_ref.dtype)
            k_ref[0, :, sl] = ks * _head_rms_scale(ks) * gk_ref[:, sl]

    u, va, za = seg(SEG_U), seg(SEG_VA), seg(SEG_ZA)
    row = lax.broadcasted_iota(jnp.int32, (CHUNK, CHUNK), 0)
    col = lax.broadcasted_iota(jnp.int32, (CHUNK, CHUNK), 1)
    mix_mask = (row >= col) & ((row // mix_block) == (col // mix_block))
    for g in range(N_GROUPS_A):
        sl = slice(g * GROUP_A, (g + 1) * GROUP_A)
        vs = va[:, sl]
        vn = vs * _rsqrt_mean(vs, GROUP_A) * gva_ref[:, sl]
        if maybe_vn_ref:
            maybe_vn_ref[0][0, :, sl] = vn
        vn16 = vn.astype(BF16)
        wm = jnp.where(mix_mask, wm_ref[g], 0.0).astype(BF16)
        mixed = jnp.concatenate(
            [jnp.dot(wm, vn16[c * CHUNK:(c + 1) * CHUNK], preferred_element_type=F32) + bm_ref[g]
             for c in range(tm // CHUNK)], axis=0)
        a = u[:, sl] * mixed
        a = a * _rsqrt_mean(a, GROUP_A) * goa_ref[:, sl]
        a_ref[0, :, sl] = (a * _silu(za[:, sl])).astype(a_ref.dtype)


def _proj_call(x, gn, win16, wqkvt16, wm, bm, gva, goa, gq, gk, *, mix_block, q_dtype,
               feature_major, emit_vn):
    g, r, _ = x.shape
    tm = min(TM_PROJ, r)
    row = lambda w: pl.BlockSpec((1, tm, w), lambda b, i: (b, i, 0))
    if feature_major:
        qkv_shape, qkv_spec = (g, D_B, r), pl.BlockSpec((1, D_B, tm), lambda b, i: (b, 0, i))
    else:
        qkv_shape, qkv_spec = (g, r, D_B), row(D_B)
    out_shape = [jax.ShapeDtypeStruct((g, r, D_A), BF16),
                 jax.ShapeDtypeStruct(qkv_shape, q_dtype),
                 jax.ShapeDtypeStruct(qkv_shape, F32),
                 jax.ShapeDtypeStruct(qkv_shape, F32),
                 jax.ShapeDtypeStruct((g, r, D_B), BF16)]
    out_specs = [row(D_A), qkv_spec, qkv_spec, qkv_spec, row(D_B)]
    if emit_vn:
        out_shape.append(jax.ShapeDtypeStruct((g, r, D_A), F32))
        out_specs.append(row(D_A))
    return pl.pallas_call(
        functools.partial(_proj_kernel, mix_block=mix_block, feature_major=feature_major),
        grid=(g, r // tm),
        in_specs=[row(D_MODEL), _full((1, D_MODEL)), _full(win16.shape), _full(wqkvt16.shape),
                  _full(wm.shape), _full(bm.shape), _full((1, D_A)), _full((1, D_A)),
                  _full((1, D_B)), _full((1, D_B)), _full((D_B, 1)), _full((D_B, 1))],
        out_specs=out_specs,
        out_shape=out_shape,
        compiler_params=pltpu.CompilerParams(dimension_semantics=("parallel", "parallel"),
                                             vmem_limit_bytes=VMEM_LIMIT),
        name="proj",
    )(x, gn, win16, wqkvt16, wm, bm, gva, goa, gq, gk, gq.reshape(D_B, 1), gk.reshape(D_B, 1))


def _gated_head_norm(o, zb, gob):
    return o * _head_rms_scale(o) * gob * _silu(zb)


V_ROWS = HEAD_DIM + 16
LOOKAHEAD = 1
NEW_PAD = LANES


def _sample_attention(q_ref, kn_ref, vn_ref, ckt_ref, cvt_ref, bc_ref, bn_ref, zb_ref, gob_ref,
                      b_ref):
    t = q_ref.shape[1]
    rows = N_HEADS_B * t
    r_i = lax.broadcasted_iota(jnp.int32, (rows, D_B), 0)
    c_i = lax.broadcasted_iota(jnp.int32, (rows, D_B), 1)
    head_mask = (r_i // t) == (c_i // HEAD_DIM)
    q_rep = jnp.concatenate([q_ref[0]] * N_HEADS_B, axis=0)
    q_bd = jnp.where(head_mask, q_rep, 0.0).astype(BF16)

    pad = jnp.zeros((NEW_PAD - t, D_B), F32)
    k_new = jnp.concatenate([kn_ref[0], pad], axis=0).astype(BF16)
    v_new = jnp.concatenate([vn_ref[0], pad], axis=0).astype(BF16)

    s_c = jnp.dot(q_bd, ckt_ref[0].astype(BF16), preferred_element_type=F32) + bc_ref[...]
    s_n = lax.dot_general(q_bd, k_new, NT, preferred_element_type=F32) + bn_ref[...]
    m = jnp.maximum(jnp.max(s_c, axis=-1, keepdims=True), jnp.max(s_n, axis=-1, keepdims=True))
    p_c = jnp.exp2(s_c - m)
    p_n = jnp.exp2(s_n - m)
    l = jnp.sum(p_c, axis=-1, keepdims=True) + jnp.sum(p_n, axis=-1, keepdims=True)
    o = (lax.dot_general(p_c.astype(BF16), cvt_ref[0].astype(BF16), NT,
                         preferred_element_type=F32)
         + jnp.dot(p_n.astype(BF16), v_new, preferred_element_type=F32)) / l
    o = jnp.where(head_mask, o, 0.0)
    o_tok = o[0:t]
    for hd in range(1, N_HEADS_B):
        o_tok = o_tok + o[hd * t:(hd + 1) * t]
    for p in range(D_B // LANES):
        sl = slice(p * LANES, (p + 1) * LANES)
        b_ref[0, :, sl] = _gated_head_norm(
            o_tok[:, sl], zb_ref[0, :, sl].astype(F32), gob_ref[:, sl]).astype(b_ref.dtype)


def _attn_kernel(qt_ref, kt_ref, vt_ref, bias_ref, zb_ref, gobcol_ref,
                 sq_ref, skn_ref, svn_ref, ckt_ref, cvt_ref, bc_ref, bn_ref, szb_ref, gob_ref,
                 b_ref, sb_ref, k16, v1, qm, m_sc, acc_sc):
    i = pl.program_id(1)
    _sample_attention(sq_ref, skn_ref, svn_ref, ckt_ref, cvt_ref, bc_ref, bn_ref, szb_ref,
                      gob_ref, sb_ref)

    @pl.when(i == 0)
    def _():
        ones = jnp.ones((V_ROWS - HEAD_DIM, TQ), BF16)
        for t in range(k16.shape[0]):
            cols = slice(t * TQ, (t + 1) * TQ)
            k16[t] = kt_ref[0, :, cols].T.astype(BF16)
            vt = vt_ref[0, :, cols].astype(BF16)
            for h in range(N_HEADS_B):
                v1[t, h, :HEAD_DIM, :] = vt[h * HEAD_DIM:(h + 1) * HEAD_DIM]
                v1[t, h, HEAD_DIM:, :] = ones

    row_lo = lax.broadcasted_iota(jnp.int32, (LANES, TQ), 0) < HEAD_DIM
    for p in range(D_B // LANES):
        q2 = qt_ref[0, p * LANES:(p + 1) * LANES, :]
        qm[2 * p] = jnp.where(row_lo, q2, jnp.zeros_like(q2))
        qm[2 * p + 1] = jnp.where(row_lo, jnp.zeros_like(q2), q2)
    m_sc[...] = jnp.full(m_sc.shape, NEG, F32)
    acc_sc[...] = jnp.zeros(acc_sc.shape, F32)

    def body(t, carry):
        bias = bias_ref[jnp.minimum(i - t, bias_ref.shape[0] - 1)]

        def scores(h):
            p = h // 2
            return jnp.dot(k16[t, :, p * LANES:(p + 1) * LANES], qm[h],
                           preferred_element_type=F32)

        sts = [scores(h) + bias for h in range(N_HEADS_B)]
        tops = [jnp.max(st, axis=0, keepdims=True) for st in sts]
        for h in range(N_HEADS_B):
            floor = functools.reduce(jnp.minimum, tops[h:h + 1 + LOOKAHEAD],
                                     jnp.full_like(tops[0], NEG))
            m_old = m_sc[h]
            m_new = jnp.maximum(jnp.maximum(m_old, tops[h]), floor)
            alpha = jnp.exp2(m_old - m_new)
            pt = jnp.exp2(sts[h] - m_new).astype(BF16)
            acc_sc[h] = alpha * acc_sc[h] + jnp.dot(v1[t, h], pt, preferred_element_type=F32)
            m_sc[h] = m_new
        return carry

    lax.fori_loop(0, i + 1, body, 0)

    for p in range(D_B // LANES):
        halves = []
        for h in (2 * p, 2 * p + 1):
            acc = acc_sc[h]
            o = acc[:HEAD_DIM] / acc[HEAD_DIM:HEAD_DIM + 1]
            ms = jnp.sum(o * o, axis=0, keepdims=True) * (1.0 / HEAD_DIM)
            halves.append(o * lax.rsqrt(ms + EPS) * gobcol_ref[h * HEAD_DIM:(h + 1) * HEAD_DIM])
        sl = slice(p * LANES, (p + 1) * LANES)
        pair = jnp.concatenate(halves, axis=0).T
        b_ref[0, :, sl] = (pair * _silu(zb_ref[0, :, sl].astype(F32))).astype(b_ref.dtype)


def _attn_call(qt, kt, vt, zb, sq, sk_new, sv_new, cache_kt, cache_vt, szb, gob):
    bsz, _, s = qt.shape
    dbs, t, _ = sq.shape
    wb = cache_kt.shape[2]
    nt = s // TQ
    assert dbs == bsz * nt, "one sample sequence per prompt query tile"
    near = max(w for w, _ in PATTERNS[:-1])
    nkd = (near + TQ - 1) // TQ + 2
    qk = np.arange(TQ)[None, :] - np.arange(TQ)[:, None]
    bias = jnp.asarray(np.stack([_log2_bias(kd * TQ + qk) for kd in range(nkd)]))
    tq = np.tile(np.arange(t), N_HEADS_B)[:, None]
    bias_c = jnp.asarray(_log2_bias(wb + tq - np.arange(wb)[None, :]))
    d_new = np.where(np.arange(NEW_PAD)[None, :] < t, tq - np.arange(NEW_PAD)[None, :], -1)
    bias_n = jnp.asarray(_log2_bias(d_new))

    qtile = pl.BlockSpec((1, D_B, TQ), lambda b, i: (b, 0, i))
    tile = pl.BlockSpec((1, TQ, D_B), lambda b, i: (b, i, 0))
    whole = pl.BlockSpec((1, D_B, s), lambda b, i: (b, 0, 0))
    tok = pl.BlockSpec((1, t, D_B), lambda b, i: (b * nt + i, 0, 0))
    win = pl.BlockSpec((1, D_B, wb), lambda b, i: (b * nt + i, 0, 0))
    return pl.pallas_call(
        _attn_kernel,
        grid=(bsz, nt),
        in_specs=[qtile, whole, whole, _full(bias.shape), tile, _full((D_B, 1)),
                  tok, tok, tok, win, win, _full(bias_c.shape), _full(bias_n.shape), tok,
                  _full((1, D_B))],
        out_specs=[tile, tok],
        out_shape=[jax.ShapeDtypeStruct((bsz, s, D_B), BF16),
                   jax.ShapeDtypeStruct((dbs, t, D_B), F32)],
        scratch_shapes=[pltpu.VMEM((nt, TQ, D_B), BF16),
                        pltpu.VMEM((nt, N_HEADS_B, V_ROWS, TQ), BF16),
                        pltpu.VMEM((N_HEADS_B, LANES, TQ), BF16),
                        pltpu.VMEM((N_HEADS_B, 1, TQ), F32),
                        pltpu.VMEM((N_HEADS_B, V_ROWS, TQ), F32)],
        compiler_params=pltpu.CompilerParams(dimension_semantics=("parallel", "arbitrary"),
                                             vmem_limit_bytes=VMEM_LIMIT_ATTN),
        name="attn",
    )(qt, kt, vt, bias, zb, gob.reshape(D_B, 1),
      sq, sk_new, sv_new, cache_kt, cache_vt, bias_c, bias_n, szb, gob)


def _finish_kernel(x_ref, a_ref, b_ref, p_ref, wout_ref, wple_ref, wgate_ref, gple_ref, y_ref):
    h = (x_ref[...]
         + jnp.dot(a_ref[...].astype(BF16), wout_ref[:D_A, :], preferred_element_type=F32)
         + jnp.dot(b_ref[...].astype(BF16), wout_ref[D_A:, :], preferred_element_type=F32))
    gate = jax.nn.sigmoid(jnp.dot(h.astype(BF16), wgate_ref[...], preferred_element_type=F32))
    e = jnp.dot(p_ref[...].astype(BF16), wple_ref[...], preferred_element_type=F32)
    e = e * _rsqrt_mean(e, D_MODEL) * gple_ref[...]
    y_ref[...] = h + gate * e


def _finish_call(x, a, b, p, wout16, wple16, wgate16, gple):
    n = x.shape[0]
    tm = min(TM_FIN, n)
    row = lambda w: pl.BlockSpec((tm, w), lambda i: (i, 0))
    return pl.pallas_call(
        _finish_kernel,
        grid=(n // tm,),
        in_specs=[row(D_MODEL), row(D_A), row(D_B), row(D_PLE), _full(wout16.shape),
                  _full(wple16.shape), _full(wgate16.shape), _full((1, D_MODEL))],
        out_specs=row(D_MODEL),
        out_shape=jax.ShapeDtypeStruct((n, D_MODEL), F32),
        compiler_params=pltpu.CompilerParams(dimension_semantics=("parallel",),
                                             vmem_limit_bytes=VMEM_LIMIT),
        name="finish",
    )(x, a, b, p, wout16, wple16, wgate16, gple)


def _feature_major(c):
    b, p, h, dh = c.shape
    return jnp.transpose(c, (0, 2, 3, 1)).reshape(b, h * dh, p)


def _position_major(ct, h):
    b, hd, p = ct.shape
    return jnp.transpose(ct.reshape(b, h, hd // h, p), (0, 3, 1, 2))


def kernel(x_prompt, x_sample, cache_k, cache_v, p_prompt, p_sample, g_norm, w_in, w_s, b_s,
           g_va, g_oa, g_q, g_k, g_ob, w_out, w_ple, g_ple, w_ple_gate):
    depth = w_in.shape[0]
    assert depth == 1, "single-layer step"
    bsz, seq, _ = x_prompt.shape
    dbs, dseq, _ = x_sample.shape
    assert seq % TQ == 0 and seq <= max(w for w, _ in PATTERNS) and CHUNK % dseq == 0
    i = 0

    gn = g_norm[i].reshape(1, D_MODEL)
    win16 = w_in[i].astype(BF16)
    wqkvt16 = jnp.transpose(w_in[i][:, SEG_Q * SEG:(SEG_VB + 1) * SEG]).astype(BF16)
    gva = g_va[i].reshape(1, D_A)
    goa = g_oa[i].reshape(1, D_A)
    gq = jnp.tile(g_q[i], N_HEADS_B).reshape(1, D_B)
    gk = jnp.tile(g_k[i], N_HEADS_B).reshape(1, D_B)
    gob = g_ob[i].reshape(1, D_B)
    wout16 = w_out[i].astype(BF16)
    wple16 = w_ple[i].astype(BF16)
    wgate16 = w_ple_gate[i].astype(BF16)
    gple = g_ple[i].reshape(1, D_MODEL)

    a_p, qt_p, kt_p, vt_p, zb_p = _proj_call(
        x_prompt, gn, win16, wqkvt16, w_s[i], b_s[i][:, :, None], gva, goa, gq, gk,
        mix_block=CHUNK, q_dtype=BF16, feature_major=True, emit_vn=False)
    rep = CHUNK // dseq
    wm_s = jnp.tile(w_s[i][:, :dseq, :dseq], (1, rep, rep))
    bm_s = jnp.tile(b_s[i][:, :dseq], (1, rep))[:, :, None]
    xs = x_sample.reshape(1, dbs * dseq, D_MODEL)
    a_s, q_s, k_s, v_s, zb_s, vn_s = _proj_call(
        xs, gn, win16, wqkvt16, wm_s, bm_s, gva, goa, gq, gk,
        mix_block=dseq, q_dtype=F32, feature_major=False, emit_vn=True)

    tok = lambda z: z.reshape(dbs, dseq, D_B)
    b_p, b_s_ = _attn_call(qt_p, kt_p, vt_p, zb_p, tok(q_s), tok(k_s), tok(v_s),
                           _feature_major(cache_k[i]), _feature_major(cache_v[i]), tok(zb_s), gob)

    y_p = _finish_call(x_prompt.reshape(bsz * seq, D_MODEL), a_p.reshape(bsz * seq, D_A),
                       b_p.reshape(bsz * seq, D_B), p_prompt[i].reshape(bsz * seq, D_PLE),
                       wout16, wple16, wgate16, gple)
    y_s = _finish_call(xs.reshape(dbs * dseq, D_MODEL), a_s.reshape(dbs * dseq, D_A),
                       b_s_.reshape(dbs * dseq, D_B), p_sample[i].reshape(dbs * dseq, D_PLE),
                       wout16, wple16, wgate16, gple)

    hshape = (N_HEADS_B, HEAD_DIM)
    return (y_p.reshape(bsz, seq, D_MODEL),
            y_s.reshape(dbs, dseq, D_MODEL),
            _position_major(kt_p, N_HEADS_B)[None],
            _position_major(vt_p, N_HEADS_B)[None],
            k_s.reshape(1, dbs, dseq, *hshape),
            v_s.reshape(1, dbs, dseq, *hshape),
            vn_s.reshape(1, dbs, dseq, D_A))
```

```python
import functools

import numpy as np
import jax
import jax.numpy as jnp
from jax import lax
from jax.experimental import pallas as pl
from jax.experimental.pallas import tpu as pltpu

D_MODEL = 1024
D_A = 512
N_GROUPS_A = 4
GROUP_A = 128
CHUNK = 128
D_B = 512
N_HEADS_B = 8
HEAD_DIM = 64
PATTERNS = ((128, 1), (512, 4), (2048, 16))
D_PLE = 256
EPS = 1e-6
NEG = -1e30
SCALE = HEAD_DIM ** -0.5
Q_SCALE = SCALE * float(np.log2(np.e))
SEG = 512
SEG_U, SEG_VA, SEG_ZA, SEG_Q, SEG_K, SEG_VB, SEG_ZB = range(7)

LANES = 128
VMEM_LIMIT = 48 * 1024 * 1024
VMEM_LIMIT_ATTN = 56 * 1024 * 1024

TM_PROJ = 512
TM_FIN = 512
TQ = 256

BF16 = jnp.bfloat16
F32 = jnp.float32
NT = (((1,), (1,)), ((), ()))


def _multiplicity(d):
    c = np.zeros(d.shape, np.int32)
    for w, r in PATTERNS:
        c += ((d >= 0) & (d <= w) & (d % r == 0)).astype(np.int32)
    return c


def _log2_bias(d):
    c = _multiplicity(d)
    return np.where(c > 0, np.log2(np.maximum(c, 1).astype(np.float64)), NEG).astype(np.float32)


def _rsqrt_mean(x, n):
    return lax.rsqrt(jnp.sum(x * x, axis=-1, keepdims=True) * (1.0 / n) + EPS)


def _head_rms_scale(x):
    lane_lo = lax.broadcasted_iota(jnp.int32, x.shape, 1) < HEAD_DIM
    x2 = x * x
    lo = jnp.sum(jnp.where(lane_lo, x2, 0.0), axis=-1, keepdims=True)
    hi = jnp.sum(jnp.where(lane_lo, 0.0, x2), axis=-1, keepdims=True)
    ms = jnp.where(lane_lo, lo, hi) * (1.0 / HEAD_DIM)
    return lax.rsqrt(ms + EPS)


def _silu(z):
    return z * jax.nn.sigmoid(z)


def _full(shape):
    return pl.BlockSpec(shape, lambda *_: (0,) * len(shape))


def _head_norm_feature_major(xt, gcol_ref):
    m = xt.shape[1]
    x3 = xt.reshape(N_HEADS_B, HEAD_DIM, m)
    ms = jnp.sum(x3 * x3, axis=1, keepdims=True) * (1.0 / HEAD_DIM)
    g3 = gcol_ref[...].reshape(N_HEADS_B, HEAD_DIM, 1)
    return (x3 * lax.rsqrt(ms + EPS) * g3).reshape(D_B, m)


def _proj_kernel(x_ref, gn_ref, win_ref, wqkvt_ref, wm_ref, bm_ref, gva_ref, goa_ref, gq_ref,
                 gk_ref, gqcol_ref, gkcol_ref, a_ref, q_ref, k_ref, v_ref, zb_ref, *maybe_vn_ref,
                 mix_block, feature_major):
    tm = x_ref.shape[1]
    x = x_ref[0]
    h = (x * _rsqrt_mean(x, D_MODEL) * gn_ref[...]).astype(BF16)

    def seg(i):
        return jnp.dot(h, win_ref[:, i * SEG:(i + 1) * SEG], preferred_element_type=F32)

    zb_ref[0] = seg(SEG_ZB).astype(zb_ref.dtype)
    if feature_major:
        qkvt = lax.dot_general(wqkvt_ref[...], h, NT, preferred_element_type=F32)
        q_ref[0] = (_head_norm_feature_major(qkvt[:D_B], gqcol_ref) * Q_SCALE).astype(q_ref.dtype)
        k_ref[0] = _head_norm_feature_major(qkvt[D_B:2 * D_B], gkcol_ref)
        v_ref[0] = qkvt[2 * D_B:]
    else:
        q, k = seg(SEG_Q), seg(SEG_K)
        v_ref[0] = seg(SEG_VB)
        for j in range(D_B // LANES):
            sl = slice(j * LANES, (j + 1) * LANES)
            qs, ks = q[:, sl], k[:, sl]
            q_ref[0, :, sl] = (qs * _head_rms_scale(qs) * gq_ref[:, sl] * Q_SCALE).astype(q_ref.dtype)
            k_ref[0, :, sl] = ks * _head_rms_scale(ks) * gk_ref[:, sl]

    u, va, za = seg(SEG_U), seg(SEG_VA), seg(SEG_ZA)
    row = lax.broadcasted_iota(jnp.int32, (CHUNK, CHUNK), 0)
    col = lax.broadcasted_iota(jnp.int32, (CHUNK, CHUNK), 1)
    mix_mask = (row >= col) & ((row // mix_block) == (col // mix_block))
    for g in range(N_GROUPS_A):
        sl = slice(g * GROUP_A, (g + 1) * GROUP_A)
        vs = va[:, sl]
        vn = vs * _rsqrt_mean(vs, GROUP_A) * gva_ref[:, sl]
        if maybe_vn_ref:
            maybe_vn_ref[0][0, :, sl] = vn
        vn16 = vn.astype(BF16)
        wm = jnp.where(mix_mask, wm_ref[g], 0.0).astype(BF16)
        mixed = jnp.concatenate(
            [jnp.dot(wm, vn16[c * CHUNK:(c + 1) * CHUNK], preferred_element_type=F32) + bm_ref[g]
             for c in range(tm // CHUNK)], axis=0)
        a = u[:, sl] * mixed
        a = a * _rsqrt_mean(a, GROUP_A) * goa_ref[:, sl]
        a_ref[0, :, sl] = (a * _silu(za[:, sl])).astype(a_ref.dtype)


def _proj_call(x, gn, win16, wqkvt16, wm, bm, gva, goa, gq, gk, *, mix_block, q_dtype,
               feature_major, emit_vn):
    g, r, _ = x.shape
    tm = min(TM_PROJ, r)
    row = lambda w: pl.BlockSpec((1, tm, w), lambda b, i: (b, i, 0))
    if feature_major:
        qkv_shape, qkv_spec = (g, D_B, r), pl.BlockSpec((1, D_B, tm), lambda b, i: (b, 0, i))
    else:
        qkv_shape, qkv_spec = (g, r, D_B), row(D_B)
    out_shape = [jax.ShapeDtypeStruct((g, r, D_A), BF16),
                 jax.ShapeDtypeStruct(qkv_shape, q_dtype),
                 jax.ShapeDtypeStruct(qkv_shape, F32),
                 jax.ShapeDtypeStruct(qkv_shape, F32),
                 jax.ShapeDtypeStruct((g, r, D_B), BF16)]
    out_specs = [row(D_A), qkv_spec, qkv_spec, qkv_spec, row(D_B)]
    if emit_vn:
        out_shape.append(jax.ShapeDtypeStruct((g, r, D_A), F32))
        out_specs.append(row(D_A))
    return pl.pallas_call(
        functools.partial(_proj_kernel, mix_block=mix_block, feature_major=feature_major),
        grid=(g, r // tm),
        in_specs=[row(D_MODEL), _full((1, D_MODEL)), _full(win16.shape), _full(wqkvt16.shape),
                  _full(wm.shape), _full(bm.shape), _full((1, D_A)), _full((1, D_A)),
                  _full((1, D_B)), _full((1, D_B)), _full((D_B, 1)), _full((D_B, 1))],
        out_specs=out_specs,
        out_shape=out_shape,
        compiler_params=pltpu.CompilerParams(dimension_semantics=("parallel", "parallel"),
                                             vmem_limit_bytes=VMEM_LIMIT),
        name="proj",
    )(x, gn, win16, wqkvt16, wm, bm, gva, goa, gq, gk, gq.reshape(D_B, 1), gk.reshape(D_B, 1))


def _gated_head_norm(o, zb, gob):
    return o * _head_rms_scale(o) * gob * _silu(zb)


V_ROWS = HEAD_DIM + 16
LOOKAHEAD = 1
NEW_PAD = LANES


def _sample_attention(q_ref, kn_ref, vn_ref, ckt_ref, cvt_ref, bc_ref, bn_ref, zb_ref, gob_ref,
                      b_ref):
    t = q_ref.shape[1]
    rows = N_HEADS_B * t
    r_i = lax.broadcasted_iota(jnp.int32, (rows, D_B), 0)
    c_i = lax.broadcasted_iota(jnp.int32, (rows, D_B), 1)
    head_mask = (r_i // t) == (c_i // HEAD_DIM)
    q_rep = jnp.concatenate([q_ref[0]] * N_HEADS_B, axis=0)
    q_bd = jnp.where(head_mask, q_rep, 0.0).astype(BF16)

    pad = jnp.zeros((NEW_PAD - t, D_B), F32)
    k_new = jnp.concatenate([kn_ref[0], pad], axis=0).astype(BF16)
    v_new = jnp.concatenate([vn_ref[0], pad], axis=0).astype(BF16)

    s_c = jnp.dot(q_bd.astype(F32), ckt_ref[0], preferred_element_type=F32) + bc_ref[...]
    s_n = lax.dot_general(q_bd, k_new, NT, preferred_element_type=F32) + bn_ref[...]
    m = jnp.maximum(jnp.max(s_c, axis=-1, keepdims=True), jnp.max(s_n, axis=-1, keepdims=True))
    p_c = jnp.exp2(s_c - m)
    p_n = jnp.exp2(s_n - m)
    l = jnp.sum(p_c, axis=-1, keepdims=True) + jnp.sum(p_n, axis=-1, keepdims=True)
    o = (lax.dot_general(p_c, cvt_ref[0], NT, preferred_element_type=F32)
         + jnp.dot(p_n.astype(BF16), v_new, preferred_element_type=F32)) / l
    o = jnp.where(head_mask, o, 0.0)
    o_tok = o[0:t]
    for hd in range(1, N_HEADS_B):
        o_tok = o_tok + o[hd * t:(hd + 1) * t]
    for p in range(D_B // LANES):
        sl = slice(p * LANES, (p + 1) * LANES)
        b_ref[0, :, sl] = _gated_head_norm(
            o_tok[:, sl], zb_ref[0, :, sl].astype(F32), gob_ref[:, sl]).astype(b_ref.dtype)


SAFE_SPREAD = 100.0


def _attn_kernel(qt_ref, kt_ref, vt_ref, bias_ref, zb_ref, gobcol_ref,
                 sq_ref, skn_ref, svn_ref, ckt_ref, cvt_ref, bc_ref, bn_ref, szb_ref, gob_ref,
                 b_ref, sb_ref, k16, v1, qm, knorm, m_sc, acc_sc):
    i = pl.program_id(1)
    _sample_attention(sq_ref, skn_ref, svn_ref, ckt_ref, cvt_ref, bc_ref, bn_ref, szb_ref,
                      gob_ref, sb_ref)

    @pl.when(i == 0)
    def _():
        ones = jnp.ones((V_ROWS - HEAD_DIM, TQ), BF16)
        ksq_max = jnp.zeros((N_HEADS_B, 1, 1), F32)
        for t in range(k16.shape[0]):
            cols = slice(t * TQ, (t + 1) * TQ)
            kt = kt_ref[0, :, cols]
            k16[t] = kt.T.astype(BF16)
            ksq = jnp.sum((kt * kt).reshape(N_HEADS_B, HEAD_DIM, TQ), axis=1, keepdims=True)
            ksq_max = jnp.maximum(ksq_max, jnp.max(ksq, axis=2, keepdims=True))
            vt = vt_ref[0, :, cols].astype(BF16)
            for h in range(N_HEADS_B):
                v1[t, h, :HEAD_DIM, :] = vt[h * HEAD_DIM:(h + 1) * HEAD_DIM]
                v1[t, h, HEAD_DIM:, :] = ones
        knorm[...] = jnp.broadcast_to(jnp.sqrt(ksq_max), knorm.shape)

    row_lo = lax.broadcasted_iota(jnp.int32, (LANES, TQ), 0) < HEAD_DIM
    for p in range(D_B // LANES):
        q2 = qt_ref[0, p * LANES:(p + 1) * LANES, :]
        qm[2 * p] = jnp.where(row_lo, q2, jnp.zeros_like(q2))
        qm[2 * p + 1] = jnp.where(row_lo, jnp.zeros_like(q2), q2)
    acc_sc[...] = jnp.zeros(acc_sc.shape, F32)

    qf = qt_ref[0].astype(F32).reshape(N_HEADS_B, HEAD_DIM, TQ)
    qnorm = jnp.sqrt(jnp.sum(qf * qf, axis=1, keepdims=True))
    spread = qnorm * knorm[...]
    safe = jnp.max(spread) * 2.0 + 4.0 < SAFE_SPREAD

    def scores(t, h):
        p = h // 2
        bias = bias_ref[jnp.minimum(i - t, bias_ref.shape[0] - 1)]
        return jnp.dot(k16[t, :, p * LANES:(p + 1) * LANES], qm[h],
                       preferred_element_type=F32) + bias

    @pl.when(safe)
    def _():
        m_sc[...] = spread + float(np.log2(len(PATTERNS)))

        def accumulate(tiles):
            units = [(t, h) for t in tiles for h in range(N_HEADS_B)]
            sts = [scores(t, h) for t, h in units]
            pv = [0.0] * N_HEADS_B
            for u, (t, h) in enumerate(units):
                ref8 = jnp.broadcast_to(m_sc[h], (8, TQ))
                if u + 1 < len(units):
                    ref8 = jnp.maximum(ref8, jnp.minimum(sts[u + 1][:8], NEG))
                pt = jnp.exp2((sts[u].reshape(TQ // 8, 8, TQ) - ref8[None]).reshape(TQ, TQ))
                pv[h] = pv[h] + jnp.dot(v1[t, h], pt.astype(BF16), preferred_element_type=F32)
            for h in range(N_HEADS_B):
                acc_sc[h] += pv[h]

        def pair_body(j, carry):
            accumulate([2 * j, 2 * j + 1])
            return carry

        lax.fori_loop(0, (i + 1) // 2, pair_body, 0)

        @pl.when(i % 2 == 0)
        def _():
            accumulate([i])

    @pl.when(jnp.logical_not(safe))
    def _():
        m_sc[...] = jnp.full(m_sc.shape, NEG, F32)

        def body(t, carry):
            sts = [scores(t, h) for h in range(N_HEADS_B)]
            tops = [jnp.max(st, axis=0, keepdims=True) for st in sts]
            for h in range(N_HEADS_B):
                floor = functools.reduce(jnp.minimum, tops[h:h + 1 + LOOKAHEAD],
                                         jnp.full_like(tops[0], NEG))
                m_old = m_sc[h]
                m_new = jnp.maximum(jnp.maximum(m_old, tops[h]), floor)
                alpha = jnp.exp2(m_old - m_new)
                pt = jnp.exp2(sts[h] - m_new).astype(BF16)
                acc_sc[h] = alpha * acc_sc[h] + jnp.dot(v1[t, h], pt,
                                                        preferred_element_type=F32)
                m_sc[h] = m_new
            return carry

        lax.fori_loop(0, i + 1, body, 0)

    for p in range(D_B // LANES):
        halves = []
        for h in (2 * p, 2 * p + 1):
            acc = acc_sc[h]
            o = acc[:HEAD_DIM] / acc[HEAD_DIM:HEAD_DIM + 1]
            ms = jnp.sum(o * o, axis=0, keepdims=True) * (1.0 / HEAD_DIM)
            halves.append(o * lax.rsqrt(ms + EPS) * gobcol_ref[h * HEAD_DIM:(h + 1) * HEAD_DIM])
        sl = slice(p * LANES, (p + 1) * LANES)
        pair = jnp.concatenate(halves, axis=0).T
        b_ref[0, :, sl] = (pair * _silu(zb_ref[0, :, sl].astype(F32))).astype(b_ref.dtype)


def _attn_call(qt, kt, vt, zb, sq, sk_new, sv_new, cache_kt, cache_vt, szb, gob):
    bsz, _, s = qt.shape
    dbs, t, _ = sq.shape
    wb = cache_kt.shape[2]
    nt = s // TQ
    assert dbs == bsz * nt, "one sample sequence per prompt query tile"
    near = max(w for w, _ in PATTERNS[:-1])
    nkd = (near + TQ - 1) // TQ + 2
    qk = np.arange(TQ)[None, :] - np.arange(TQ)[:, None]
    bias = jnp.asarray(np.stack([_log2_bias(kd * TQ + qk) for kd in range(nkd)]))
    tq = np.tile(np.arange(t), N_HEADS_B)[:, None]
    bias_c = jnp.asarray(_log2_bias(wb + tq - np.arange(wb)[None, :]))
    d_new = np.where(np.arange(NEW_PAD)[None, :] < t, tq - np.arange(NEW_PAD)[None, :], -1)
    bias_n = jnp.asarray(_log2_bias(d_new))

    qtile = pl.BlockSpec((1, D_B, TQ), lambda b, i: (b, 0, i))
    tile = pl.BlockSpec((1, TQ, D_B), lambda b, i: (b, i, 0))
    whole = pl.BlockSpec((1, D_B, s), lambda b, i: (b, 0, 0))
    tok = pl.BlockSpec((1, t, D_B), lambda b, i: (b * nt + i, 0, 0))
    win = pl.BlockSpec((1, D_B, wb), lambda b, i: (b * nt + i, 0, 0))
    return pl.pallas_call(
        _attn_kernel,
        grid=(bsz, nt),
        in_specs=[qtile, whole, whole, _full(bias.shape), tile, _full((D_B, 1)),
                  tok, tok, tok, win, win, _full(bias_c.shape), _full(bias_n.shape), tok,
                  _full((1, D_B))],
        out_specs=[tile, tok],
        out_shape=[jax.ShapeDtypeStruct((bsz, s, D_B), BF16),
                   jax.ShapeDtypeStruct((dbs, t, D_B), F32)],
        scratch_shapes=[pltpu.VMEM((nt, TQ, D_B), BF16),
                        pltpu.VMEM((nt, N_HEADS_B, V_ROWS, TQ), BF16),
                        pltpu.VMEM((N_HEADS_B, LANES, TQ), BF16),
                        pltpu.VMEM((N_HEADS_B, 1, TQ), F32),
                        pltpu.VMEM((N_HEADS_B, 1, TQ), F32),
                        pltpu.VMEM((N_HEADS_B, V_ROWS, TQ), F32)],
        compiler_params=pltpu.CompilerParams(dimension_semantics=("parallel", "arbitrary"),
                                             vmem_limit_bytes=VMEM_LIMIT_ATTN),
        name="attn",
    )(qt, kt, vt, bias, zb, gob.reshape(D_B, 1),
      sq, sk_new, sv_new, cache_kt, cache_vt, bias_c, bias_n, szb, gob)


def _finish_kernel(x_ref, a_ref, b_ref, p_ref, wout_ref, wple_ref, wgate_ref, gple_ref, y_ref):
    h = (x_ref[...]
         + jnp.dot(a_ref[...].astype(BF16), wout_ref[:D_A, :], preferred_element_type=F32)
         + jnp.dot(b_ref[...].astype(BF16), wout_ref[D_A:, :], preferred_element_type=F32))
    gate = jax.nn.sigmoid(jnp.dot(h.astype(BF16), wgate_ref[...], preferred_element_type=F32))
    e = jnp.dot(p_ref[...].astype(BF16), wple_ref[...], preferred_element_type=F32)
    e = e * _rsqrt_mean(e, D_MODEL) * gple_ref[...]
    y_ref[...] = h + gate * e


def _finish_call(x, a, b, p, wout16, wple16, wgate16, gple):
    n = x.shape[0]
    tm = min(TM_FIN, n)
    row = lambda w: pl.BlockSpec((tm, w), lambda i: (i, 0))
    return pl.pallas_call(
        _finish_kernel,
        grid=(n // tm,),
        in_specs=[row(D_MODEL), row(D_A), row(D_B), row(D_PLE), _full(wout16.shape),
                  _full(wple16.shape), _full(wgate16.shape), _full((1, D_MODEL))],
        out_specs=row(D_MODEL),
        out_shape=jax.ShapeDtypeStruct((n, D_MODEL), F32),
        compiler_params=pltpu.CompilerParams(dimension_semantics=("parallel",),
                                             vmem_limit_bytes=VMEM_LIMIT),
        name="finish",
    )(x, a, b, p, wout16, wple16, wgate16, gple)


def _feature_major(c):
    b, p, h, dh = c.shape
    return jnp.transpose(c, (0, 2, 3, 1)).reshape(b, h * dh, p)


def _position_major(ct, h):
    b, hd, p = ct.shape
    return jnp.transpose(ct.reshape(b, h, hd // h, p), (0, 3, 1, 2))


def kernel(x_prompt, x_sample, cache_k, cache_v, p_prompt, p_sample, g_norm, w_in, w_s, b_s,
           g_va, g_oa, g_q, g_k, g_ob, w_out, w_ple, g_ple, w_ple_gate):
    depth = w_in.shape[0]
    assert depth == 1, "single-layer step"
    bsz, seq, _ = x_prompt.shape
    dbs, dseq, _ = x_sample.shape
    assert seq % TQ == 0 and seq <= max(w for w, _ in PATTERNS) and CHUNK % dseq == 0
    i = 0

    gn = g_norm[i].reshape(1, D_MODEL)
    win16 = w_in[i].astype(BF16)
    wqkvt16 = jnp.transpose(w_in[i][:, SEG_Q * SEG:(SEG_VB + 1) * SEG]).astype(BF16)
    gva = g_va[i].reshape(1, D_A)
    goa = g_oa[i].reshape(1, D_A)
    gq = jnp.tile(g_q[i], N_HEADS_B).reshape(1, D_B)
    gk = jnp.tile(g_k[i], N_HEADS_B).reshape(1, D_B)
    gob = g_ob[i].reshape(1, D_B)
    wout16 = w_out[i].astype(BF16)
    wple16 = w_ple[i].astype(BF16)
    wgate16 = w_ple_gate[i].astype(BF16)
    gple = g_ple[i].reshape(1, D_MODEL)

    a_p, qt_p, kt_p, vt_p, zb_p = _proj_call(
        x_prompt, gn, win16, wqkvt16, w_s[i], b_s[i][:, :, None], gva, goa, gq, gk,
        mix_block=CHUNK, q_dtype=BF16, feature_major=True, emit_vn=False)
    rep = CHUNK // dseq
    wm_s = jnp.tile(w_s[i][:, :dseq, :dseq], (1, rep, rep))
    bm_s = jnp.tile(b_s[i][:, :dseq], (1, rep))[:, :, None]
    xs = x_sample.reshape(1, dbs * dseq, D_MODEL)
    a_s, q_s, k_s, v_s, zb_s, vn_s = _proj_call(
        xs, gn, win16, wqkvt16, wm_s, bm_s, gva, goa, gq, gk,
        mix_block=dseq, q_dtype=F32, feature_major=False, emit_vn=True)

    tok = lambda z: z.reshape(dbs, dseq, D_B)
    b_p, b_s_ = _attn_call(qt_p, kt_p, vt_p, zb_p, tok(q_s), tok(k_s), tok(v_s),
                           _feature_major(cache_k[i]), _feature_major(cache_v[i]), tok(zb_s), gob)

    y_p = _finish_call(x_prompt.reshape(bsz * seq, D_MODEL), a_p.reshape(bsz * seq, D_A),
                       b_p.reshape(bsz * seq, D_B), p_prompt[i].reshape(bsz * seq, D_PLE),
                       wout16, wple16, wgate16, gple)
    y_s = _finish_call(xs.reshape(dbs * dseq, D_MODEL), a_s.reshape(dbs * dseq, D_A),
                       b_s_.reshape(dbs * dseq, D_B), p_sample[i].reshape(dbs * dseq, D_PLE),
                       wout16, wple16, wgate16, gple)

    hshape = (N_HEADS_B, HEAD_DIM)
    return (y_p.reshape(bsz, seq, D_MODEL),
            y_s.reshape(dbs, dseq, D_MODEL),
            _position_major(kt_p, N_HEADS_B)[None],
            _position_major(vt_p, N_HEADS_B)[None],
            k_s.reshape(1, dbs, dseq, *hshape),
            v_s.reshape(1, dbs, dseq, *hshape),
            vn_s.reshape(1, dbs, dseq, D_A))
```

```python
import functools

import numpy as np
import jax
import jax.numpy as jnp
from jax import lax
from jax.experimental import pallas as pl
from jax.experimental.pallas import tpu as pltpu

D_MODEL = 1024
D_A = 512
N_GROUPS_A = 4
GROUP_A = 128
CHUNK = 128
D_B = 512
N_HEADS_B = 8
HEAD_DIM = 64
PATTERNS = ((128, 1), (512, 4), (2048, 16))
D_PLE = 256
EPS = 1e-6
NEG = -1e30
SCALE = HEAD_DIM ** -0.5
Q_SCALE = SCALE * float(np.log2(np.e))
SEG = 512
SEG_U, SEG_VA, SEG_ZA, SEG_Q, SEG_K, SEG_VB, SEG_ZB = range(7)

LANES = 128
VMEM_LIMIT = 48 * 1024 * 1024
VMEM_LIMIT_ATTN = 56 * 1024 * 1024

TM_PROJ = 512
TM_FIN = 1024
FIN_CHUNK = 256
TQ = 256

BF16 = jnp.bfloat16
F32 = jnp.float32
NT = (((1,), (1,)), ((), ()))


def _multiplicity(d):
    c = np.zeros(d.shape, np.int32)
    for w, r in PATTERNS:
        c += ((d >= 0) & (d <= w) & (d % r == 0)).astype(np.int32)
    return c


def _log2_bias(d):
    c = _multiplicity(d)
    return np.where(c > 0, np.log2(np.maximum(c, 1).astype(np.float64)), NEG).astype(np.float32)


def _rsqrt_mean(x, n):
    return lax.rsqrt(jnp.sum(x * x, axis=-1, keepdims=True) * (1.0 / n) + EPS)


def _head_rms_scale(x):
    lane_lo = lax.broadcasted_iota(jnp.int32, x.shape, 1) < HEAD_DIM
    x2 = x * x
    lo = jnp.sum(jnp.where(lane_lo, x2, 0.0), axis=-1, keepdims=True)
    hi = jnp.sum(jnp.where(lane_lo, 0.0, x2), axis=-1, keepdims=True)
    ms = jnp.where(lane_lo, lo, hi) * (1.0 / HEAD_DIM)
    return lax.rsqrt(ms + EPS)


def _silu(z):
    return z * jax.nn.sigmoid(z)


def _full(shape):
    return pl.BlockSpec(shape, lambda *_: (0,) * len(shape))


def _head_norm_feature_major(xt, gcol_ref):
    m = xt.shape[1]
    x3 = xt.reshape(N_HEADS_B, HEAD_DIM, m)
    ms = jnp.sum(x3 * x3, axis=1, keepdims=True) * (1.0 / HEAD_DIM)
    g3 = gcol_ref[...].reshape(N_HEADS_B, HEAD_DIM, 1)
    return (x3 * lax.rsqrt(ms + EPS) * g3).reshape(D_B, m)


def _proj_kernel(x_ref, gn_ref, win_ref, wqkvt_ref, wm_ref, bm_ref, gva_ref, goa_ref, gq_ref,
                 gk_ref, gqcol_ref, gkcol_ref, a_ref, q_ref, k_ref, v_ref, zb_ref, *maybe_vn_ref,
                 mix_block, feature_major):
    tm = x_ref.shape[1]
    x = x_ref[0]
    h = (x * _rsqrt_mean(x, D_MODEL) * gn_ref[...]).astype(BF16)

    def seg(i):
        return jnp.dot(h, win_ref[:, i * SEG:(i + 1) * SEG], preferred_element_type=F32)

    zb_ref[0] = seg(SEG_ZB).astype(zb_ref.dtype)
    if feature_major:
        qkvt = lax.dot_general(wqkvt_ref[...], h, NT, preferred_element_type=F32)
        q_ref[0] = (_head_norm_feature_major(qkvt[:D_B], gqcol_ref) * Q_SCALE).astype(q_ref.dtype)
        k_ref[0] = _head_norm_feature_major(qkvt[D_B:2 * D_B], gkcol_ref)
        v_ref[0] = qkvt[2 * D_B:]
    else:
        q, k = seg(SEG_Q), seg(SEG_K)
        v_ref[0] = seg(SEG_VB)
        for j in range(D_B // LANES):
            sl = slice(j * LANES, (j + 1) * LANES)
            qs, ks = q[:, sl], k[:, sl]
            q_ref[0, :, sl] = (qs * _head_rms_scale(qs) * gq_ref[:, sl] * Q_SCALE).astype(q_ref.dtype)
            k_ref[0, :, sl] = ks * _head_rms_scale(ks) * gk_ref[:, sl]

    u, va, za = seg(SEG_U), seg(SEG_VA), seg(SEG_ZA)
    row = lax.broadcasted_iota(jnp.int32, (CHUNK, CHUNK), 0)
    col = lax.broadcasted_iota(jnp.int32, (CHUNK, CHUNK), 1)
    mix_mask = (row >= col) & ((row // mix_block) == (col // mix_block))
    for g in range(N_GROUPS_A):
        sl = slice(g * GROUP_A, (g + 1) * GROUP_A)
        vs = va[:, sl]
        vn = vs * _rsqrt_mean(vs, GROUP_A) * gva_ref[:, sl]
        if maybe_vn_ref:
            maybe_vn_ref[0][0, :, sl] = vn
        vn16 = vn.astype(BF16)
        wm = jnp.where(mix_mask, wm_ref[g], 0.0).astype(BF16)
        mixed = jnp.concatenate(
            [jnp.dot(wm, vn16[c * CHUNK:(c + 1) * CHUNK], preferred_element_type=F32) + bm_ref[g]
             for c in range(tm // CHUNK)], axis=0)
        a = u[:, sl] * mixed
        a = a * _rsqrt_mean(a, GROUP_A) * goa_ref[:, sl]
        a_ref[0, :, sl] = (a * _silu(za[:, sl])).astype(a_ref.dtype)


def _proj_call(x, gn, win16, wqkvt16, wm, bm, gva, goa, gq, gk, *, mix_block, q_dtype,
               feature_major, emit_vn):
    g, r, _ = x.shape
    tm = min(TM_PROJ, r)
    row = lambda w: pl.BlockSpec((1, tm, w), lambda b, i: (b, i, 0))
    if feature_major:
        qkv_shape, qkv_spec = (g, D_B, r), pl.BlockSpec((1, D_B, tm), lambda b, i: (b, 0, i))
    else:
        qkv_shape, qkv_spec = (g, r, D_B), row(D_B)
    out_shape = [jax.ShapeDtypeStruct((g, r, D_A), BF16),
                 jax.ShapeDtypeStruct(qkv_shape, q_dtype),
                 jax.ShapeDtypeStruct(qkv_shape, F32),
                 jax.ShapeDtypeStruct(qkv_shape, F32),
                 jax.ShapeDtypeStruct((g, r, D_B), BF16)]
    out_specs = [row(D_A), qkv_spec, qkv_spec, qkv_spec, row(D_B)]
    if emit_vn:
        out_shape.append(jax.ShapeDtypeStruct((g, r, D_A), F32))
        out_specs.append(row(D_A))
    return pl.pallas_call(
        functools.partial(_proj_kernel, mix_block=mix_block, feature_major=feature_major),
        grid=(g, r // tm),
        in_specs=[row(D_MODEL), _full((1, D_MODEL)), _full(win16.shape), _full(wqkvt16.shape),
                  _full(wm.shape), _full(bm.shape), _full((1, D_A)), _full((1, D_A)),
                  _full((1, D_B)), _full((1, D_B)), _full((D_B, 1)), _full((D_B, 1))],
        out_specs=out_specs,
        out_shape=out_shape,
        compiler_params=pltpu.CompilerParams(dimension_semantics=("parallel", "parallel"),
                                             vmem_limit_bytes=VMEM_LIMIT),
        name="proj",
    )(x, gn, win16, wqkvt16, wm, bm, gva, goa, gq, gk, gq.reshape(D_B, 1), gk.reshape(D_B, 1))


def _gated_head_norm(o, zb, gob):
    return o * _head_rms_scale(o) * gob * _silu(zb)


V_ROWS = HEAD_DIM + 16
LOOKAHEAD = 1
NEW_PAD = LANES


def _sample_head_mask(t):
    r_i = lax.broadcasted_iota(jnp.int32, (N_HEADS_B * t, D_B), 0)
    c_i = lax.broadcasted_iota(jnp.int32, (N_HEADS_B * t, D_B), 1)
    return (r_i // t) == (c_i // HEAD_DIM)


def _sample_probs(q_ref, kn_ref, ckt_ref, bc_ref, bn_ref):
    t = q_ref.shape[1]
    q_rep = jnp.concatenate([q_ref[0]] * N_HEADS_B, axis=0)
    q_bd = jnp.where(_sample_head_mask(t), q_rep, 0.0).astype(BF16)
    pad = jnp.zeros((NEW_PAD - t, D_B), F32)
    k_new = jnp.concatenate([kn_ref[0], pad], axis=0).astype(BF16)

    s_c = jnp.dot(q_bd.astype(F32), ckt_ref[0], preferred_element_type=F32) + bc_ref[...]
    s_n = lax.dot_general(q_bd, k_new, NT, preferred_element_type=F32) + bn_ref[...]
    m = jnp.maximum(jnp.max(s_c, axis=-1, keepdims=True), jnp.max(s_n, axis=-1, keepdims=True))
    p_c = jnp.exp2(s_c - m)
    p_n = jnp.exp2(s_n - m)
    l = jnp.sum(p_c, axis=-1, keepdims=True) + jnp.sum(p_n, axis=-1, keepdims=True)
    return p_c, p_n, l


def _sample_values(p_c, p_n, l, vn_ref, cvt_ref, zb_ref, gob_ref, b_ref):
    t = vn_ref.shape[1]
    pad = jnp.zeros((NEW_PAD - t, D_B), F32)
    v_new = jnp.concatenate([vn_ref[0], pad], axis=0).astype(BF16)
    o = (lax.dot_general(p_c, cvt_ref[0], NT, preferred_element_type=F32)
         + jnp.dot(p_n.astype(BF16), v_new, preferred_element_type=F32)) / l
    o = jnp.where(_sample_head_mask(t), o, 0.0)
    o_tok = o[0:t]
    for hd in range(1, N_HEADS_B):
        o_tok = o_tok + o[hd * t:(hd + 1) * t]
    for p in range(D_B // LANES):
        sl = slice(p * LANES, (p + 1) * LANES)
        b_ref[0, :, sl] = _gated_head_norm(
            o_tok[:, sl], zb_ref[0, :, sl].astype(F32), gob_ref[:, sl]).astype(b_ref.dtype)


SAFE_SPREAD = 100.0


def _attn_kernel(qt_ref, kt_ref, vt_ref, bias_ref, zb_ref, gobcol_ref,
                 sq_ref, skn_ref, svn_ref, ckt_ref, cvt_ref, bc_ref, bn_ref, szb_ref, gob_ref,
                 b_ref, sb_ref, k16, v1, qm, knorm, m_sc, acc_sc):
    i = pl.program_id(1)
    sample_p = _sample_probs(sq_ref, skn_ref, ckt_ref, bc_ref, bn_ref)

    @pl.when(i == 0)
    def _():
        ones = jnp.ones((V_ROWS - HEAD_DIM, TQ), BF16)
        ksq_max = jnp.zeros((N_HEADS_B, 1, 1), F32)
        for t in range(k16.shape[0]):
            cols = slice(t * TQ, (t + 1) * TQ)
            kt = kt_ref[0, :, cols]
            k16[t] = kt.T.astype(BF16)
            ksq = jnp.sum((kt * kt).reshape(N_HEADS_B, HEAD_DIM, TQ), axis=1, keepdims=True)
            ksq_max = jnp.maximum(ksq_max, jnp.max(ksq, axis=2, keepdims=True))
            vt = vt_ref[0, :, cols].astype(BF16)
            for h in range(N_HEADS_B):
                v1[t, h, :HEAD_DIM, :] = vt[h * HEAD_DIM:(h + 1) * HEAD_DIM]
                v1[t, h, HEAD_DIM:, :] = ones
        knorm[...] = jnp.broadcast_to(jnp.sqrt(ksq_max), knorm.shape)

    row_lo = lax.broadcasted_iota(jnp.int32, (LANES, TQ), 0) < HEAD_DIM
    for p in range(D_B // LANES):
        q2 = qt_ref[0, p * LANES:(p + 1) * LANES, :]
        qm[2 * p] = jnp.where(row_lo, q2, jnp.zeros_like(q2))
        qm[2 * p + 1] = jnp.where(row_lo, jnp.zeros_like(q2), q2)
    acc_sc[...] = jnp.zeros(acc_sc.shape, F32)

    qf = qt_ref[0].astype(F32).reshape(N_HEADS_B, HEAD_DIM, TQ)
    qnorm = jnp.sqrt(jnp.sum(qf * qf, axis=1, keepdims=True))
    spread = qnorm * knorm[...]
    safe = jnp.max(spread) * 2.0 + 4.0 < SAFE_SPREAD

    def scores(t, h):
        p = h // 2
        bias = bias_ref[jnp.minimum(i - t, bias_ref.shape[0] - 1)]
        return jnp.dot(k16[t, :, p * LANES:(p + 1) * LANES], qm[h],
                       preferred_element_type=F32) + bias

    @pl.when(safe)
    def _():
        m_sc[...] = spread + float(np.log2(len(PATTERNS)))

        def accumulate(tiles):
            units = [(t, h) for t in tiles for h in range(N_HEADS_B)]
            sts = [scores(t, h) for t, h in units]
            pv = [0.0] * N_HEADS_B
            for u, (t, h) in enumerate(units):
                ref8 = jnp.broadcast_to(m_sc[h], (8, TQ))
                if u + 1 < len(units):
                    ref8 = jnp.maximum(ref8, jnp.minimum(sts[u + 1][:8], NEG))
                pt = jnp.exp2((sts[u].reshape(TQ // 8, 8, TQ) - ref8[None]).reshape(TQ, TQ))
                pv[h] = pv[h] + jnp.dot(v1[t, h], pt.astype(BF16), preferred_element_type=F32)
            for h in range(N_HEADS_B):
                acc_sc[h] += pv[h]

        def pair_body(j, carry):
            accumulate([2 * j, 2 * j + 1])
            return carry

        lax.fori_loop(0, (i + 1) // 2, pair_body, 0)

        @pl.when(i % 2 == 0)
        def _():
            accumulate([i])

    @pl.when(jnp.logical_not(safe))
    def _():
        m_sc[...] = jnp.full(m_sc.shape, NEG, F32)

        def body(t, carry):
            sts = [scores(t, h) for h in range(N_HEADS_B)]
            tops = [jnp.max(st, axis=0, keepdims=True) for st in sts]
            for h in range(N_HEADS_B):
                floor = functools.reduce(jnp.minimum, tops[h:h + 1 + LOOKAHEAD],
                                         jnp.full_like(tops[0], NEG))
                m_old = m_sc[h]
                m_new = jnp.maximum(jnp.maximum(m_old, tops[h]), floor)
                alpha = jnp.exp2(m_old - m_new)
                pt = jnp.exp2(sts[h] - m_new).astype(BF16)
                acc_sc[h] = alpha * acc_sc[h] + jnp.dot(v1[t, h], pt,
                                                        preferred_element_type=F32)
                m_sc[h] = m_new
            return carry

        lax.fori_loop(0, i + 1, body, 0)

    _sample_values(*sample_p, svn_ref, cvt_ref, szb_ref, gob_ref, sb_ref)
    for p in range(D_B // LANES):
        halves = []
        for h in (2 * p, 2 * p + 1):
            acc = acc_sc[h]
            o = acc[:HEAD_DIM] / acc[HEAD_DIM:HEAD_DIM + 1]
            ms = jnp.sum(o * o, axis=0, keepdims=True) * (1.0 / HEAD_DIM)
            halves.append(o * lax.rsqrt(ms + EPS) * gobcol_ref[h * HEAD_DIM:(h + 1) * HEAD_DIM])
        sl = slice(p * LANES, (p + 1) * LANES)
        pair = jnp.concatenate(halves, axis=0).T
        b_ref[0, :, sl] = (pair * _silu(zb_ref[0, :, sl].astype(F32))).astype(b_ref.dtype)


def _attn_call(qt, kt, vt, zb, sq, sk_new, sv_new, cache_kt, cache_vt, szb, gob):
    bsz, _, s = qt.shape
    dbs, t, _ = sq.shape
    wb = cache_kt.shape[2]
    nt = s // TQ
    assert dbs == bsz * nt, "one sample sequence per prompt query tile"
    near = max(w for w, _ in PATTERNS[:-1])
    nkd = (near + TQ - 1) // TQ + 2
    qk = np.arange(TQ)[None, :] - np.arange(TQ)[:, None]
    bias = jnp.asarray(np.stack([_log2_bias(kd * TQ + qk) for kd in range(nkd)]))
    tq = np.tile(np.arange(t), N_HEADS_B)[:, None]
    bias_c = jnp.asarray(_log2_bias(wb + tq - np.arange(wb)[None, :]))
    d_new = np.where(np.arange(NEW_PAD)[None, :] < t, tq - np.arange(NEW_PAD)[None, :], -1)
    bias_n = jnp.asarray(_log2_bias(d_new))

    qtile = pl.BlockSpec((1, D_B, TQ), lambda b, i: (b, 0, i))
    tile = pl.BlockSpec((1, TQ, D_B), lambda b, i: (b, i, 0))
    whole = pl.BlockSpec((1, D_B, s), lambda b, i: (b, 0, 0))
    tok = pl.BlockSpec((1, t, D_B), lambda b, i: (b * nt + i, 0, 0))
    win = pl.BlockSpec((1, D_B, wb), lambda b, i: (b * nt + i, 0, 0))
    return pl.pallas_call(
        _attn_kernel,
        grid=(bsz, nt),
        in_specs=[qtile, whole, whole, _full(bias.shape), tile, _full((D_B, 1)),
                  tok, tok, tok, win, win, _full(bias_c.shape), _full(bias_n.shape), tok,
                  _full((1, D_B))],
        out_specs=[tile, tok],
        out_shape=[jax.ShapeDtypeStruct((bsz, s, D_B), BF16),
                   jax.ShapeDtypeStruct((dbs, t, D_B), F32)],
        scratch_shapes=[pltpu.VMEM((nt, TQ, D_B), BF16),
                        pltpu.VMEM((nt, N_HEADS_B, V_ROWS, TQ), BF16),
                        pltpu.VMEM((N_HEADS_B, LANES, TQ), BF16),
                        pltpu.VMEM((N_HEADS_B, 1, TQ), F32),
                        pltpu.VMEM((N_HEADS_B, 1, TQ), F32),
                        pltpu.VMEM((N_HEADS_B, V_ROWS, TQ), F32)],
        compiler_params=pltpu.CompilerParams(dimension_semantics=("parallel", "arbitrary"),
                                             vmem_limit_bytes=VMEM_LIMIT_ATTN),
        name="attn",
    )(qt, kt, vt, bias, zb, gob.reshape(D_B, 1),
      sq, sk_new, sv_new, cache_kt, cache_vt, bias_c, bias_n, szb, gob)


def _finish_kernel(x_ref, a_ref, b_ref, p_ref, wout_ref, wple_ref, wgate_ref, gple_ref, y_ref):
    h = (x_ref[...]
         + jnp.dot(a_ref[...].astype(BF16), wout_ref[:D_A, :], preferred_element_type=F32)
         + jnp.dot(b_ref[...].astype(BF16), wout_ref[D_A:, :], preferred_element_type=F32))
    e = jnp.dot(p_ref[...].astype(BF16), wple_ref[...], preferred_element_type=F32)
    e = e * _rsqrt_mean(e, D_MODEL) * gple_ref[...]
    h16 = h.astype(BF16)
    for c in range(0, D_MODEL, FIN_CHUNK):
        sl = slice(c, c + FIN_CHUNK)
        gate = jax.nn.sigmoid(jnp.dot(h16, wgate_ref[:, sl], preferred_element_type=F32))
        y_ref[:, sl] = h[:, sl] + gate * e[:, sl]


def _finish_call(x, a, b, p, wout16, wple16, wgate16, gple):
    n = x.shape[0]
    tm = min(TM_FIN, n)
    row = lambda w: pl.BlockSpec((tm, w), lambda i: (i, 0))
    return pl.pallas_call(
        _finish_kernel,
        grid=(n // tm,),
        in_specs=[row(D_MODEL), row(D_A), row(D_B), row(D_PLE), _full(wout16.shape),
                  _full(wple16.shape), _full(wgate16.shape), _full((1, D_MODEL))],
        out_specs=row(D_MODEL),
        out_shape=jax.ShapeDtypeStruct((n, D_MODEL), F32),
        compiler_params=pltpu.CompilerParams(dimension_semantics=("parallel",),
                                             vmem_limit_bytes=VMEM_LIMIT),
        name="finish",
    )(x, a, b, p, wout16, wple16, wgate16, gple)


def _feature_major(c):
    b, p, h, dh = c.shape
    return jnp.transpose(c, (0, 2, 3, 1)).reshape(b, h * dh, p)


def _position_major(ct, h):
    b, hd, p = ct.shape
    return jnp.transpose(ct.reshape(b, h, hd // h, p), (0, 3, 1, 2))


def kernel(x_prompt, x_sample, cache_k, cache_v, p_prompt, p_sample, g_norm, w_in, w_s, b_s,
           g_va, g_oa, g_q, g_k, g_ob, w_out, w_ple, g_ple, w_ple_gate):
    depth = w_in.shape[0]
    assert depth == 1, "single-layer step"
    bsz, seq, _ = x_prompt.shape
    dbs, dseq, _ = x_sample.shape
    assert seq % TQ == 0 and seq <= max(w for w, _ in PATTERNS) and CHUNK % dseq == 0
    i = 0

    gn = g_norm[i].reshape(1, D_MODEL)
    win16 = w_in[i].astype(BF16)
    wqkvt16 = jnp.transpose(w_in[i][:, SEG_Q * SEG:(SEG_VB + 1) * SEG]).astype(BF16)
    gva = g_va[i].reshape(1, D_A)
    goa = g_oa[i].reshape(1, D_A)
    gq = jnp.tile(g_q[i], N_HEADS_B).reshape(1, D_B)
    gk = jnp.tile(g_k[i], N_HEADS_B).reshape(1, D_B)
    gob = g_ob[i].reshape(1, D_B)
    wout16 = w_out[i].astype(BF16)
    wple16 = w_ple[i].astype(BF16)
    wgate16 = w_ple_gate[i].astype(BF16)
    gple = g_ple[i].reshape(1, D_MODEL)

    a_p, qt_p, kt_p, vt_p, zb_p = _proj_call(
        x_prompt, gn, win16, wqkvt16, w_s[i], b_s[i][:, :, None], gva, goa, gq, gk,
        mix_block=CHUNK, q_dtype=BF16, feature_major=True, emit_vn=False)
    rep = CHUNK // dseq
    wm_s = jnp.tile(w_s[i][:, :dseq, :dseq], (1, rep, rep))
    bm_s = jnp.tile(b_s[i][:, :dseq], (1, rep))[:, :, None]
    xs = x_sample.reshape(1, dbs * dseq, D_MODEL)
    a_s, q_s, k_s, v_s, zb_s, vn_s = _proj_call(
        xs, gn, win16, wqkvt16, wm_s, bm_s, gva, goa, gq, gk,
        mix_block=dseq, q_dtype=F32, feature_major=False, emit_vn=True)

    tok = lambda z: z.reshape(dbs, dseq, D_B)
    b_p, b_s_ = _attn_call(qt_p, kt_p, vt_p, zb_p, tok(q_s), tok(k_s), tok(v_s),
                           _feature_major(cache_k[i]), _feature_major(cache_v[i]), tok(zb_s), gob)

    y_p = _finish_call(x_prompt.reshape(bsz * seq, D_MODEL), a_p.reshape(bsz * seq, D_A),
                       b_p.reshape(bsz * seq, D_B), p_prompt[i].reshape(bsz * seq, D_PLE),
                       wout16, wple16, wgate16, gple)
    y_s = _finish_call(xs.reshape(dbs * dseq, D_MODEL), a_s.reshape(dbs * dseq, D_A),
                       b_s_.reshape(dbs * dseq, D_B), p_sample[i].reshape(dbs * dseq, D_PLE),
                       wout16, wple16, wgate16, gple)

    hshape = (N_HEADS_B, HEAD_DIM)
    return (y_p.reshape(bsz, seq, D_MODEL),
            y_s.reshape(dbs, dseq, D_MODEL),
            _position_major(kt_p, N_HEADS_B)[None],
            _position_major(vt_p, N_HEADS_B)[None],
            k_s.reshape(1, dbs, dseq, *hshape),
            v_s.reshape(1, dbs, dseq, *hshape),
            vn_s.reshape(1, dbs, dseq, D_A))
```

```python
import functools

import numpy as np
import jax
import jax.numpy as jnp
from jax import lax
from jax.experimental import pallas as pl
from jax.experimental.pallas import tpu as pltpu

D_MODEL = 1024
D_A = 512
N_GROUPS_A = 4
GROUP_A = 128
CHUNK = 128
D_B = 512
N_HEADS_B = 8
HEAD_DIM = 64
PATTERNS = ((128, 1), (512, 4), (2048, 16))
D_PLE = 256
EPS = 1e-6
NEG = -1e30
SCALE = HEAD_DIM ** -0.5
Q_SCALE = SCALE * float(np.log2(np.e))
SEG = 512
SEG_U, SEG_VA, SEG_ZA, SEG_Q, SEG_K, SEG_VB, SEG_ZB = range(7)

LANES = 128
VMEM_LIMIT = 48 * 1024 * 1024
VMEM_LIMIT_ATTN = 56 * 1024 * 1024

TM_PROJ = 1024
PROJ_SUB = 256
TM_FIN = 1024
FIN_SUB = 256
FIN_CHUNK = 256
TQ = 256

BF16 = jnp.bfloat16
F32 = jnp.float32
NT = (((1,), (1,)), ((), ()))


def _multiplicity(d):
    c = np.zeros(d.shape, np.int32)
    for w, r in PATTERNS:
        c += ((d >= 0) & (d <= w) & (d % r == 0)).astype(np.int32)
    return c


def _log2_bias(d):
    c = _multiplicity(d)
    return np.where(c > 0, np.log2(np.maximum(c, 1).astype(np.float64)), NEG).astype(np.float32)


def _rsqrt_mean(x, n):
    return lax.rsqrt(jnp.sum(x * x, axis=-1, keepdims=True) * (1.0 / n) + EPS)


def _head_rms_scale(x):
    lane_lo = lax.broadcasted_iota(jnp.int32, x.shape, 1) < HEAD_DIM
    x2 = x * x
    lo = jnp.sum(jnp.where(lane_lo, x2, 0.0), axis=-1, keepdims=True)
    hi = jnp.sum(jnp.where(lane_lo, 0.0, x2), axis=-1, keepdims=True)
    ms = jnp.where(lane_lo, lo, hi) * (1.0 / HEAD_DIM)
    return lax.rsqrt(ms + EPS)


def _silu(z):
    return z * jax.nn.sigmoid(z)


def _full(shape):
    return pl.BlockSpec(shape, lambda *_: (0,) * len(shape))


def _head_norm_feature_major(xt, gcol_ref):
    m = xt.shape[1]
    x3 = xt.reshape(N_HEADS_B, HEAD_DIM, m)
    ms = jnp.sum(x3 * x3, axis=1, keepdims=True) * (1.0 / HEAD_DIM)
    g3 = gcol_ref[...].reshape(N_HEADS_B, HEAD_DIM, 1)
    return (x3 * lax.rsqrt(ms + EPS) * g3).reshape(D_B, m)


def _proj_kernel(x_ref, gn_ref, win_ref, wqkvt_ref, wm_ref, bm_ref, gva_ref, goa_ref, gq_ref,
                 gk_ref, gqcol_ref, gkcol_ref, a_ref, q_ref, k_ref, v_ref, zb_ref, *maybe_vn_ref,
                 mix_block, feature_major):
    row = lax.broadcasted_iota(jnp.int32, (CHUNK, CHUNK), 0)
    col = lax.broadcasted_iota(jnp.int32, (CHUNK, CHUNK), 1)
    mix_mask = (row >= col) & ((row // mix_block) == (col // mix_block))
    wms = [jnp.where(mix_mask, wm_ref[g], 0.0).astype(BF16) for g in range(N_GROUPS_A)]

    for r0 in range(0, x_ref.shape[1], PROJ_SUB):
        rows = slice(r0, r0 + PROJ_SUB)
        x = x_ref[0, rows, :]
        h = (x * _rsqrt_mean(x, D_MODEL) * gn_ref[...]).astype(BF16)

        def seg(i):
            return jnp.dot(h, win_ref[:, i * SEG:(i + 1) * SEG], preferred_element_type=F32)

        zb_ref[0, rows, :] = seg(SEG_ZB).astype(zb_ref.dtype)
        if feature_major:
            qkvt = lax.dot_general(wqkvt_ref[...], h, NT, preferred_element_type=F32)
            q_ref[0, :, rows] = (_head_norm_feature_major(qkvt[:D_B], gqcol_ref)
                                 * Q_SCALE).astype(q_ref.dtype)
            k_ref[0, :, rows] = _head_norm_feature_major(qkvt[D_B:2 * D_B], gkcol_ref)
            v_ref[0, :, rows] = qkvt[2 * D_B:]
        else:
            q, k = seg(SEG_Q), seg(SEG_K)
            v_ref[0, rows, :] = seg(SEG_VB)
            for j in range(D_B // LANES):
                sl = slice(j * LANES, (j + 1) * LANES)
                qs, ks = q[:, sl], k[:, sl]
                q_ref[0, rows, sl] = (qs * _head_rms_scale(qs) * gq_ref[:, sl]
                                      * Q_SCALE).astype(q_ref.dtype)
                k_ref[0, rows, sl] = ks * _head_rms_scale(ks) * gk_ref[:, sl]

        u, va, za = seg(SEG_U), seg(SEG_VA), seg(SEG_ZA)
        for g in range(N_GROUPS_A):
            sl = slice(g * GROUP_A, (g + 1) * GROUP_A)
            vs = va[:, sl]
            vn = vs * _rsqrt_mean(vs, GROUP_A) * gva_ref[:, sl]
            if maybe_vn_ref:
                maybe_vn_ref[0][0, rows, sl] = vn
            vn16 = vn.astype(BF16)
            mixed = jnp.concatenate(
                [jnp.dot(wms[g], vn16[c * CHUNK:(c + 1) * CHUNK], preferred_element_type=F32)
                 + bm_ref[g] for c in range(PROJ_SUB // CHUNK)], axis=0)
            a = u[:, sl] * mixed
            a = a * _rsqrt_mean(a, GROUP_A) * goa_ref[:, sl]
            a_ref[0, rows, sl] = (a * _silu(za[:, sl])).astype(a_ref.dtype)


def _proj_call(x, gn, win16, wqkvt16, wm, bm, gva, goa, gq, gk, *, mix_block, q_dtype,
               feature_major, emit_vn):
    g, r, _ = x.shape
    tm = min(TM_PROJ, r)
    row = lambda w: pl.BlockSpec((1, tm, w), lambda b, i: (b, i, 0))
    if feature_major:
        qkv_shape, qkv_spec = (g, D_B, r), pl.BlockSpec((1, D_B, tm), lambda b, i: (b, 0, i))
    else:
        qkv_shape, qkv_spec = (g, r, D_B), row(D_B)
    out_shape = [jax.ShapeDtypeStruct((g, r, D_A), BF16),
                 jax.ShapeDtypeStruct(qkv_shape, q_dtype),
                 jax.ShapeDtypeStruct(qkv_shape, F32),
                 jax.ShapeDtypeStruct(qkv_shape, F32),
                 jax.ShapeDtypeStruct((g, r, D_B), BF16)]
    out_specs = [row(D_A), qkv_spec, qkv_spec, qkv_spec, row(D_B)]
    if emit_vn:
        out_shape.append(jax.ShapeDtypeStruct((g, r, D_A), F32))
        out_specs.append(row(D_A))
    return pl.pallas_call(
        functools.partial(_proj_kernel, mix_block=mix_block, feature_major=feature_major),
        grid=(g, r // tm),
        in_specs=[row(D_MODEL), _full((1, D_MODEL)), _full(win16.shape), _full(wqkvt16.shape),
                  _full(wm.shape), _full(bm.shape), _full((1, D_A)), _full((1, D_A)),
                  _full((1, D_B)), _full((1, D_B)), _full((D_B, 1)), _full((D_B, 1))],
        out_specs=out_specs,
        out_shape=out_shape,
        compiler_params=pltpu.CompilerParams(dimension_semantics=("parallel", "parallel"),
                                             vmem_limit_bytes=VMEM_LIMIT),
        name="proj",
    )(x, gn, win16, wqkvt16, wm, bm, gva, goa, gq, gk, gq.reshape(D_B, 1), gk.reshape(D_B, 1))


def _gated_head_norm(o, zb, gob):
    return o * _head_rms_scale(o) * gob * _silu(zb)


V_ROWS = HEAD_DIM + 16
LOOKAHEAD = 1
NEW_PAD = LANES


def _sample_head_mask(t):
    r_i = lax.broadcasted_iota(jnp.int32, (N_HEADS_B * t, D_B), 0)
    c_i = lax.broadcasted_iota(jnp.int32, (N_HEADS_B * t, D_B), 1)
    return (r_i // t) == (c_i // HEAD_DIM)


def _sample_probs(q_ref, kn_ref, ckt_ref, bc_ref, bn_ref):
    t = q_ref.shape[1]
    q_rep = jnp.concatenate([q_ref[0]] * N_HEADS_B, axis=0)
    q_bd = jnp.where(_sample_head_mask(t), q_rep, 0.0).astype(BF16)
    pad = jnp.zeros((NEW_PAD - t, D_B), F32)
    k_new = jnp.concatenate([kn_ref[0], pad], axis=0).astype(BF16)

    s_c = jnp.dot(q_bd.astype(F32), ckt_ref[0], preferred_element_type=F32) + bc_ref[...]
    s_n = lax.dot_general(q_bd, k_new, NT, preferred_element_type=F32) + bn_ref[...]
    m = jnp.maximum(jnp.max(s_c, axis=-1, keepdims=True), jnp.max(s_n, axis=-1, keepdims=True))
    p_c = jnp.exp2(s_c - m)
    p_n = jnp.exp2(s_n - m)
    l = jnp.sum(p_c, axis=-1, keepdims=True) + jnp.sum(p_n, axis=-1, keepdims=True)
    return p_c, p_n, l


def _sample_values(p_c, p_n, l, vn_ref, cvt_ref, zb_ref, gob_ref, b_ref):
    t = vn_ref.shape[1]
    pad = jnp.zeros((NEW_PAD - t, D_B), F32)
    v_new = jnp.concatenate([vn_ref[0], pad], axis=0).astype(BF16)
    o = (lax.dot_general(p_c, cvt_ref[0], NT, preferred_element_type=F32)
         + jnp.dot(p_n.astype(BF16), v_new, preferred_element_type=F32)) / l
    o = jnp.where(_sample_head_mask(t), o, 0.0)
    o_tok = o[0:t]
    for hd in range(1, N_HEADS_B):
        o_tok = o_tok + o[hd * t:(hd + 1) * t]
    for p in range(D_B // LANES):
        sl = slice(p * LANES, (p + 1) * LANES)
        b_ref[0, :, sl] = _gated_head_norm(
            o_tok[:, sl], zb_ref[0, :, sl].astype(F32), gob_ref[:, sl]).astype(b_ref.dtype)


SAFE_SPREAD = 100.0


def _attn_kernel(qt_ref, kt_ref, vt_ref, bias_ref, zb_ref, gobcol_ref,
                 sq_ref, skn_ref, svn_ref, ckt_ref, cvt_ref, bc_ref, bn_ref, szb_ref, gob_ref,
                 b_ref, sb_ref, k16, v1, qm, knorm, m_sc, acc_sc):
    i = pl.program_id(1)
    sample_p = _sample_probs(sq_ref, skn_ref, ckt_ref, bc_ref, bn_ref)

    @pl.when(i == 0)
    def _():
        ones = jnp.ones((V_ROWS - HEAD_DIM, TQ), BF16)
        ksq_max = jnp.zeros((N_HEADS_B, 1, 1), F32)
        for t in range(k16.shape[0]):
            cols = slice(t * TQ, (t + 1) * TQ)
            kt = kt_ref[0, :, cols]
            k16[t] = kt.T.astype(BF16)
            ksq = jnp.sum((kt * kt).reshape(N_HEADS_B, HEAD_DIM, TQ), axis=1, keepdims=True)
            ksq_max = jnp.maximum(ksq_max, jnp.max(ksq, axis=2, keepdims=True))
            vt = vt_ref[0, :, cols].astype(BF16)
            for h in range(N_HEADS_B):
                v1[t, h, :HEAD_DIM, :] = vt[h * HEAD_DIM:(h + 1) * HEAD_DIM]
                v1[t, h, HEAD_DIM:, :] = ones
        knorm[...] = jnp.broadcast_to(jnp.sqrt(ksq_max), knorm.shape)

    row_lo = lax.broadcasted_iota(jnp.int32, (LANES, TQ), 0) < HEAD_DIM
    for p in range(D_B // LANES):
        q2 = qt_ref[0, p * LANES:(p + 1) * LANES, :]
        qm[2 * p] = jnp.where(row_lo, q2, jnp.zeros_like(q2))
        qm[2 * p + 1] = jnp.where(row_lo, jnp.zeros_like(q2), q2)
    acc_sc[...] = jnp.zeros(acc_sc.shape, F32)

    qf = qt_ref[0].astype(F32).reshape(N_HEADS_B, HEAD_DIM, TQ)
    qnorm = jnp.sqrt(jnp.sum(qf * qf, axis=1, keepdims=True))
    spread = qnorm * knorm[...]
    safe = jnp.max(spread) * 2.0 + 4.0 < SAFE_SPREAD

    def scores(t, h):
        p = h // 2
        bias = bias_ref[jnp.minimum(i - t, bias_ref.shape[0] - 1)]
        return jnp.dot(k16[t, :, p * LANES:(p + 1) * LANES], qm[h],
                       preferred_element_type=F32) + bias

    @pl.when(safe)
    def _():
        m_sc[...] = spread + float(np.log2(len(PATTERNS)))

        def accumulate(tiles):
            units = [(t, h) for t in tiles for h in range(N_HEADS_B)]
            sts = [scores(t, h) for t, h in units]
            pv = [0.0] * N_HEADS_B
            for u, (t, h) in enumerate(units):
                ref8 = jnp.broadcast_to(m_sc[h], (8, TQ))
                if u + 1 < len(units):
                    ref8 = jnp.maximum(ref8, jnp.minimum(sts[u + 1][:8], NEG))
                pt = jnp.exp2((sts[u].reshape(TQ // 8, 8, TQ) - ref8[None]).reshape(TQ, TQ))
                pv[h] = pv[h] + jnp.dot(v1[t, h], pt.astype(BF16), preferred_element_type=F32)
            for h in range(N_HEADS_B):
                acc_sc[h] += pv[h]

        def pair_body(j, carry):
            accumulate([2 * j, 2 * j + 1])
            return carry

        lax.fori_loop(0, (i + 1) // 2, pair_body, 0)

        @pl.when(i % 2 == 0)
        def _():
            accumulate([i])

    @pl.when(jnp.logical_not(safe))
    def _():
        m_sc[...] = jnp.full(m_sc.shape, NEG, F32)

        def body(t, carry):
            sts = [scores(t, h) for h in range(N_HEADS_B)]
            tops = [jnp.max(st, axis=0, keepdims=True) for st in sts]
            for h in range(N_HEADS_B):
                floor = functools.reduce(jnp.minimum, tops[h:h + 1 + LOOKAHEAD],
                                         jnp.full_like(tops[0], NEG))
                m_old = m_sc[h]
                m_new = jnp.maximum(jnp.maximum(m_old, tops[h]), floor)
                alpha = jnp.exp2(m_old - m_new)
                pt = jnp.exp2(sts[h] - m_new).astype(BF16)
                acc_sc[h] = alpha * acc_sc[h] + jnp.dot(v1[t, h], pt,
                                                        preferred_element_type=F32)
                m_sc[h] = m_new
            return carry

        lax.fori_loop(0, i + 1, body, 0)

    _sample_values(*sample_p, svn_ref, cvt_ref, szb_ref, gob_ref, sb_ref)
    for p in range(D_B // LANES):
        halves = []
        for h in (2 * p, 2 * p + 1):
            acc = acc_sc[h]
            o = acc[:HEAD_DIM] / acc[HEAD_DIM:HEAD_DIM + 1]
            ms = jnp.sum(o * o, axis=0, keepdims=True) * (1.0 / HEAD_DIM)
            halves.append(o * lax.rsqrt(ms + EPS) * gobcol_ref[h * HEAD_DIM:(h + 1) * HEAD_DIM])
        sl = slice(p * LANES, (p + 1) * LANES)
        pair = jnp.concatenate(halves, axis=0).T
        b_ref[0, :, sl] = (pair * _silu(zb_ref[0, :, sl].astype(F32))).astype(b_ref.dtype)


def _attn_call(qt, kt, vt, zb, sq, sk_new, sv_new, cache_kt, cache_vt, szb, gob):
    bsz, _, s = qt.shape
    dbs, t, _ = sq.shape
    wb = cache_kt.shape[2]
    nt = s // TQ
    assert dbs == bsz * nt, "one sample sequence per prompt query tile"
    near = max(w for w, _ in PATTERNS[:-1])
    nkd = (near + TQ - 1) // TQ + 2
    qk = np.arange(TQ)[None, :] - np.arange(TQ)[:, None]
    bias = jnp.asarray(np.stack([_log2_bias(kd * TQ + qk) for kd in range(nkd)]))
    tq = np.tile(np.arange(t), N_HEADS_B)[:, None]
    bias_c = jnp.asarray(_log2_bias(wb + tq - np.arange(wb)[None, :]))
    d_new = np.where(np.arange(NEW_PAD)[None, :] < t, tq - np.arange(NEW_PAD)[None, :], -1)
    bias_n = jnp.asarray(_log2_bias(d_new))

    qtile = pl.BlockSpec((1, D_B, TQ), lambda b, i: (b, 0, i))
    tile = pl.BlockSpec((1, TQ, D_B), lambda b, i: (b, i, 0))
    whole = pl.BlockSpec((1, D_B, s), lambda b, i: (b, 0, 0))
    tok = pl.BlockSpec((1, t, D_B), lambda b, i: (b * nt + i, 0, 0))
    win = pl.BlockSpec((1, D_B, wb), lambda b, i: (b * nt + i, 0, 0))
    return pl.pallas_call(
        _attn_kernel,
        grid=(bsz, nt),
        in_specs=[qtile, whole, whole, _full(bias.shape), tile, _full((D_B, 1)),
                  tok, tok, tok, win, win, _full(bias_c.shape), _full(bias_n.shape), tok,
                  _full((1, D_B))],
        out_specs=[tile, tok],
        out_shape=[jax.ShapeDtypeStruct((bsz, s, D_B), BF16),
                   jax.ShapeDtypeStruct((dbs, t, D_B), F32)],
        scratch_shapes=[pltpu.VMEM((nt, TQ, D_B), BF16),
                        pltpu.VMEM((nt, N_HEADS_B, V_ROWS, TQ), BF16),
                        pltpu.VMEM((N_HEADS_B, LANES, TQ), BF16),
                        pltpu.VMEM((N_HEADS_B, 1, TQ), F32),
                        pltpu.VMEM((N_HEADS_B, 1, TQ), F32),
                        pltpu.VMEM((N_HEADS_B, V_ROWS, TQ), F32)],
        compiler_params=pltpu.CompilerParams(dimension_semantics=("parallel", "arbitrary"),
                                             vmem_limit_bytes=VMEM_LIMIT_ATTN),
        name="attn",
    )(qt, kt, vt, bias, zb, gob.reshape(D_B, 1),
      sq, sk_new, sv_new, cache_kt, cache_vt, bias_c, bias_n, szb, gob)


def _finish_kernel(x_ref, a_ref, b_ref, p_ref, wout_ref, wple_ref, wgate_ref, gple_ref, y_ref):
    for r0 in range(0, x_ref.shape[0], FIN_SUB):
        rows = slice(r0, r0 + FIN_SUB)
        h = (x_ref[rows, :]
             + jnp.dot(a_ref[rows, :].astype(BF16), wout_ref[:D_A, :], preferred_element_type=F32)
             + jnp.dot(b_ref[rows, :].astype(BF16), wout_ref[D_A:, :], preferred_element_type=F32))
        e = jnp.dot(p_ref[rows, :].astype(BF16), wple_ref[...], preferred_element_type=F32)
        e = e * _rsqrt_mean(e, D_MODEL) * gple_ref[...]
        h16 = h.astype(BF16)
        for c in range(0, D_MODEL, FIN_CHUNK):
            sl = slice(c, c + FIN_CHUNK)
            gate = jax.nn.sigmoid(jnp.dot(h16, wgate_ref[:, sl], preferred_element_type=F32))
            y_ref[rows, sl] = h[:, sl] + gate * e[:, sl]


def _finish_call(x, a, b, p, wout16, wple16, wgate16, gple):
    n = x.shape[0]
    tm = min(TM_FIN, n)
    row = lambda w: pl.BlockSpec((tm, w), lambda i: (i, 0))
    return pl.pallas_call(
        _finish_kernel,
        grid=(n // tm,),
        in_specs=[row(D_MODEL), row(D_A), row(D_B), row(D_PLE), _full(wout16.shape),
                  _full(wple16.shape), _full(wgate16.shape), _full((1, D_MODEL))],
        out_specs=row(D_MODEL),
        out_shape=jax.ShapeDtypeStruct((n, D_MODEL), F32),
        compiler_params=pltpu.CompilerParams(dimension_semantics=("parallel",),
                                             vmem_limit_bytes=VMEM_LIMIT),
        name="finish",
    )(x, a, b, p, wout16, wple16, wgate16, gple)


def _feature_major(c):
    b, p, h, dh = c.shape
    return jnp.transpose(c, (0, 2, 3, 1)).reshape(b, h * dh, p)


def _position_major(ct, h):
    b, hd, p = ct.shape
    return jnp.transpose(ct.reshape(b, h, hd // h, p), (0, 3, 1, 2))


def kernel(x_prompt, x_sample, cache_k, cache_v, p_prompt, p_sample, g_norm, w_in, w_s, b_s,
           g_va, g_oa, g_q, g_k, g_ob, w_out, w_ple, g_ple, w_ple_gate):
    depth = w_in.shape[0]
    assert depth == 1, "single-layer step"
    bsz, seq, _ = x_prompt.shape
    dbs, dseq, _ = x_sample.shape
    assert seq % TQ == 0 and seq <= max(w for w, _ in PATTERNS) and CHUNK % dseq == 0
    i = 0

    gn = g_norm[i].reshape(1, D_MODEL)
    win16 = w_in[i].astype(BF16)
    wqkvt16 = jnp.transpose(w_in[i][:, SEG_Q * SEG:(SEG_VB + 1) * SEG]).astype(BF16)
    gva = g_va[i].reshape(1, D_A)
    goa = g_oa[i].reshape(1, D_A)
    gq = jnp.tile(g_q[i], N_HEADS_B).reshape(1, D_B)
    gk = jnp.tile(g_k[i], N_HEADS_B).reshape(1, D_B)
    gob = g_ob[i].reshape(1, D_B)
    wout16 = w_out[i].astype(BF16)
    wple16 = w_ple[i].astype(BF16)
    wgate16 = w_ple_gate[i].astype(BF16)
    gple = g_ple[i].reshape(1, D_MODEL)

    a_p, qt_p, kt_p, vt_p, zb_p = _proj_call(
        x_prompt, gn, win16, wqkvt16, w_s[i], b_s[i][:, :, None], gva, goa, gq, gk,
        mix_block=CHUNK, q_dtype=BF16, feature_major=True, emit_vn=False)
    rep = CHUNK // dseq
    wm_s = jnp.tile(w_s[i][:, :dseq, :dseq], (1, rep, rep))
    bm_s = jnp.tile(b_s[i][:, :dseq], (1, rep))[:, :, None]
    xs = x_sample.reshape(1, dbs * dseq, D_MODEL)
    a_s, q_s, k_s, v_s, zb_s, vn_s = _proj_call(
        xs, gn, win16, wqkvt16, wm_s, bm_s, gva, goa, gq, gk,
        mix_block=dseq, q_dtype=F32, feature_major=False, emit_vn=True)

    tok = lambda z: z.reshape(dbs, dseq, D_B)
    b_p, b_s_ = _attn_call(qt_p, kt_p, vt_p, zb_p, tok(q_s), tok(k_s), tok(v_s),
                           _feature_major(cache_k[i]), _feature_major(cache_v[i]), tok(zb_s), gob)

    y_p = _finish_call(x_prompt.reshape(bsz * seq, D_MODEL), a_p.reshape(bsz * seq, D_A),
                       b_p.reshape(bsz * seq, D_B), p_prompt[i].reshape(bsz * seq, D_PLE),
                       wout16, wple16, wgate16, gple)
    y_s = _finish_call(xs.reshape(dbs * dseq, D_MODEL), a_s.reshape(dbs * dseq, D_A),
                       b_s_.reshape(dbs * dseq, D_B), p_sample[i].reshape(dbs * dseq, D_PLE),
                       wout16, wple16, wgate16, gple)

    hshape = (N_HEADS_B, HEAD_DIM)
    return (y_p.reshape(bsz, seq, D_MODEL),
            y_s.reshape(dbs, dseq, D_MODEL),
            _position_major(kt_p, N_HEADS_B)[None],
            _position_major(vt_p, N_HEADS_B)[None],
            k_s.reshape(1, dbs, dseq, *hshape),
            v_s.reshape(1, dbs, dseq, *hshape),
            vn_s.reshape(1, dbs, dseq, D_A))
```

```python
import functools

import numpy as np
import jax
import jax.numpy as jnp
from jax import lax
from jax.experimental import pallas as pl
from jax.experimental.pallas import tpu as pltpu

D_MODEL = 1024
D_A = 512
N_GROUPS_A = 4
GROUP_A = 128
CHUNK = 128
D_B = 512
N_HEADS_B = 8
HEAD_DIM = 64
PATTERNS = ((128, 1), (512, 4), (2048, 16))
D_PLE = 256
EPS = 1e-6
NEG = -1e30
SCALE = HEAD_DIM ** -0.5
Q_SCALE = SCALE * float(np.log2(np.e))
SEG = 512
SEG_U, SEG_VA, SEG_ZA, SEG_Q, SEG_K, SEG_VB, SEG_ZB = range(7)

LANES = 128
VMEM_LIMIT = 48 * 1024 * 1024
VMEM_LIMIT_ATTN = 56 * 1024 * 1024

TM_PROJ = 1024
PROJ_SUB = 256
TM_FIN = 1024
FIN_SUB = 256
FIN_CHUNK = 256
TQ = 256

BF16 = jnp.bfloat16
F32 = jnp.float32
NT = (((1,), (1,)), ((), ()))


def _multiplicity(d):
    c = np.zeros(d.shape, np.int32)
    for w, r in PATTERNS:
        c += ((d >= 0) & (d <= w) & (d % r == 0)).astype(np.int32)
    return c


def _log2_bias(d):
    c = _multiplicity(d)
    return np.where(c > 0, np.log2(np.maximum(c, 1).astype(np.float64)), NEG).astype(np.float32)


def _rsqrt_mean(x, n):
    return lax.rsqrt(jnp.sum(x * x, axis=-1, keepdims=True) * (1.0 / n) + EPS)


def _head_rms_scale(x):
    lane_lo = lax.broadcasted_iota(jnp.int32, x.shape, 1) < HEAD_DIM
    x2 = x * x
    lo = jnp.sum(jnp.where(lane_lo, x2, 0.0), axis=-1, keepdims=True)
    hi = jnp.sum(jnp.where(lane_lo, 0.0, x2), axis=-1, keepdims=True)
    ms = jnp.where(lane_lo, lo, hi) * (1.0 / HEAD_DIM)
    return lax.rsqrt(ms + EPS)


def _silu(z):
    return z * jax.nn.sigmoid(z)


def _full(shape):
    return pl.BlockSpec(shape, lambda *_: (0,) * len(shape))


def _head_norm_feature_major(xt, gcol_ref):
    m = xt.shape[1]
    x3 = xt.reshape(N_HEADS_B, HEAD_DIM, m)
    ms = jnp.sum(x3 * x3, axis=1, keepdims=True) * (1.0 / HEAD_DIM)
    g3 = gcol_ref[...].reshape(N_HEADS_B, HEAD_DIM, 1)
    return (x3 * lax.rsqrt(ms + EPS) * g3).reshape(D_B, m)


def _proj_kernel(x_ref, gn_ref, win_ref, wqkvt_ref, wm_ref, bm_ref, gva_ref, goa_ref, gq_ref,
                 gk_ref, gqcol_ref, gkcol_ref, a_ref, q_ref, k_ref, v_ref, zb_ref, *maybe_vn_ref,
                 mix_block, feature_major):
    row = lax.broadcasted_iota(jnp.int32, (CHUNK, CHUNK), 0)
    col = lax.broadcasted_iota(jnp.int32, (CHUNK, CHUNK), 1)
    mix_mask = (row >= col) & ((row // mix_block) == (col // mix_block))
    wms = [jnp.where(mix_mask, wm_ref[g], 0.0).astype(BF16) for g in range(N_GROUPS_A)]

    for r0 in range(0, x_ref.shape[1], PROJ_SUB):
        rows = slice(r0, r0 + PROJ_SUB)
        x = x_ref[0, rows, :]
        h = (x * _rsqrt_mean(x, D_MODEL) * gn_ref[...]).astype(BF16)

        def seg(i):
            return jnp.dot(h, win_ref[:, i * SEG:(i + 1) * SEG], preferred_element_type=F32)

        zb_ref[0, rows, :] = seg(SEG_ZB).astype(zb_ref.dtype)
        if feature_major:
            qkvt = lax.dot_general(wqkvt_ref[...], h, NT, preferred_element_type=F32)
            q_ref[0, :, rows] = (_head_norm_feature_major(qkvt[:D_B], gqcol_ref)
                                 * Q_SCALE).astype(q_ref.dtype)
            k_ref[0, :, rows] = _head_norm_feature_major(qkvt[D_B:2 * D_B], gkcol_ref)
            v_ref[0, :, rows] = qkvt[2 * D_B:]
        else:
            q, k = seg(SEG_Q), seg(SEG_K)
            v_ref[0, rows, :] = seg(SEG_VB)
            for j in range(D_B // LANES):
                sl = slice(j * LANES, (j + 1) * LANES)
                qs, ks = q[:, sl], k[:, sl]
                q_ref[0, rows, sl] = (qs * _head_rms_scale(qs) * gq_ref[:, sl]
                                      * Q_SCALE).astype(q_ref.dtype)
                k_ref[0, rows, sl] = ks * _head_rms_scale(ks) * gk_ref[:, sl]

        u, va, za = seg(SEG_U), seg(SEG_VA), seg(SEG_ZA)
        for g in range(N_GROUPS_A):
            sl = slice(g * GROUP_A, (g + 1) * GROUP_A)
            vs = va[:, sl]
            vn = vs * _rsqrt_mean(vs, GROUP_A) * gva_ref[:, sl]
            if maybe_vn_ref:
                maybe_vn_ref[0][0, rows, sl] = vn
            vn16 = vn.astype(BF16)
            mixed = jnp.concatenate(
                [jnp.dot(wms[g], vn16[c * CHUNK:(c + 1) * CHUNK], preferred_element_type=F32)
                 + bm_ref[g] for c in range(PROJ_SUB // CHUNK)], axis=0)
            a = u[:, sl] * mixed
            a = a * _rsqrt_mean(a, GROUP_A) * goa_ref[:, sl]
            a_ref[0, rows, sl] = (a * _silu(za[:, sl])).astype(a_ref.dtype)


def _proj_call(x, gn, win16, wqkvt16, wm, bm, gva, goa, gq, gk, *, mix_block, q_dtype,
               feature_major, emit_vn):
    g, r, _ = x.shape
    tm = min(TM_PROJ, r)
    row = lambda w: pl.BlockSpec((1, tm, w), lambda b, i: (b, i, 0))
    if feature_major:
        qkv_shape, qkv_spec = (g, D_B, r), pl.BlockSpec((1, D_B, tm), lambda b, i: (b, 0, i))
    else:
        qkv_shape, qkv_spec = (g, r, D_B), row(D_B)
    out_shape = [jax.ShapeDtypeStruct((g, r, D_A), BF16),
                 jax.ShapeDtypeStruct(qkv_shape, q_dtype),
                 jax.ShapeDtypeStruct(qkv_shape, F32),
                 jax.ShapeDtypeStruct(qkv_shape, F32),
                 jax.ShapeDtypeStruct((g, r, D_B), BF16)]
    out_specs = [row(D_A), qkv_spec, qkv_spec, qkv_spec, row(D_B)]
    if emit_vn:
        out_shape.append(jax.ShapeDtypeStruct((g, r, D_A), F32))
        out_specs.append(row(D_A))
    return pl.pallas_call(
        functools.partial(_proj_kernel, mix_block=mix_block, feature_major=feature_major),
        grid=(g, r // tm),
        in_specs=[row(D_MODEL), _full((1, D_MODEL)), _full(win16.shape), _full(wqkvt16.shape),
                  _full(wm.shape), _full(bm.shape), _full((1, D_A)), _full((1, D_A)),
                  _full((1, D_B)), _full((1, D_B)), _full((D_B, 1)), _full((D_B, 1))],
        out_specs=out_specs,
        out_shape=out_shape,
        compiler_params=pltpu.CompilerParams(dimension_semantics=("parallel", "parallel"),
                                             vmem_limit_bytes=VMEM_LIMIT),
        name="proj",
    )(x, gn, win16, wqkvt16, wm, bm, gva, goa, gq, gk, gq.reshape(D_B, 1), gk.reshape(D_B, 1))


def _gated_head_norm(o, zb, gob):
    return o * _head_rms_scale(o) * gob * _silu(zb)


V_ROWS = HEAD_DIM + 16
LOOKAHEAD = 1
NEW_PAD = LANES


def _sample_head_mask(t):
    r_i = lax.broadcasted_iota(jnp.int32, (N_HEADS_B * t, D_B), 0)
    c_i = lax.broadcasted_iota(jnp.int32, (N_HEADS_B * t, D_B), 1)
    return (r_i // t) == (c_i // HEAD_DIM)


def _sample_probs(q_ref, kn_ref, ckt_ref, bc_ref, bn_ref):
    t = q_ref.shape[1]
    q_rep = jnp.concatenate([q_ref[0]] * N_HEADS_B, axis=0)
    q_bd = jnp.where(_sample_head_mask(t), q_rep, 0.0).astype(BF16)
    pad = jnp.zeros((NEW_PAD - t, D_B), F32)
    k_new = jnp.concatenate([kn_ref[0], pad], axis=0).astype(BF16)

    s_c = jnp.dot(q_bd.astype(F32), ckt_ref[0], preferred_element_type=F32) + bc_ref[...]
    s_n = lax.dot_general(q_bd, k_new, NT, preferred_element_type=F32) + bn_ref[...]
    m = jnp.maximum(jnp.max(s_c, axis=-1, keepdims=True), jnp.max(s_n, axis=-1, keepdims=True))
    p_c = jnp.exp2(s_c - m)
    p_n = jnp.exp2(s_n - m)
    return p_c, p_n


def _sample_values(p_c, p_n, vn_ref, cvt_ref, zb_ref, gob_ref, b_ref):
    t = vn_ref.shape[1]
    pad = jnp.zeros((NEW_PAD - t, D_B), F32)
    v_new = jnp.concatenate([vn_ref[0], pad], axis=0).astype(BF16)
    l = jnp.sum(p_c, axis=-1, keepdims=True) + jnp.sum(p_n, axis=-1, keepdims=True)
    o = (lax.dot_general(p_c, cvt_ref[0], NT, preferred_element_type=F32)
         + jnp.dot(p_n.astype(BF16), v_new, preferred_element_type=F32)) / l
    o = jnp.where(_sample_head_mask(t), o, 0.0)
    o_tok = o[0:t]
    for hd in range(1, N_HEADS_B):
        o_tok = o_tok + o[hd * t:(hd + 1) * t]
    for p in range(D_B // LANES):
        sl = slice(p * LANES, (p + 1) * LANES)
        b_ref[0, :, sl] = _gated_head_norm(
            o_tok[:, sl], zb_ref[0, :, sl].astype(F32), gob_ref[:, sl]).astype(b_ref.dtype)


SAFE_SPREAD = 100.0


def _attn_kernel(qt_ref, kt_ref, vt_ref, bias_ref, zb_ref, gobcol_ref,
                 sq_ref, skn_ref, svn_ref, ckt_ref, cvt_ref, bc_ref, bn_ref, szb_ref, gob_ref,
                 b_ref, sb_ref, k16, v1, qm, knorm, m_sc, acc_sc):
    i = pl.program_id(1)
    sample_p = _sample_probs(sq_ref, skn_ref, ckt_ref, bc_ref, bn_ref)

    row_lo = lax.broadcasted_iota(jnp.int32, (LANES, TQ), 0) < HEAD_DIM
    for p in range(D_B // LANES):
        q2 = qt_ref[0, p * LANES:(p + 1) * LANES, :]
        qm[2 * p] = jnp.where(row_lo, q2, jnp.zeros_like(q2))
        qm[2 * p + 1] = jnp.where(row_lo, jnp.zeros_like(q2), q2)
    acc_sc[...] = jnp.zeros(acc_sc.shape, F32)
    qf = qt_ref[0].astype(F32).reshape(N_HEADS_B, HEAD_DIM, TQ)
    qnorm = jnp.sqrt(jnp.sum(qf * qf, axis=1, keepdims=True))

    @pl.when(i == 0)
    def _():
        ones = jnp.ones((V_ROWS - HEAD_DIM, TQ), BF16)
        ksq_max = jnp.zeros((N_HEADS_B, 1, 1), F32)
        for t in range(k16.shape[0]):
            cols = slice(t * TQ, (t + 1) * TQ)
            kt = kt_ref[0, :, cols]
            k16[t] = kt.T.astype(BF16)
            ksq = jnp.sum((kt * kt).reshape(N_HEADS_B, HEAD_DIM, TQ), axis=1, keepdims=True)
            ksq_max = jnp.maximum(ksq_max, jnp.max(ksq, axis=2, keepdims=True))
            vt = vt_ref[0, :, cols].astype(BF16)
            for h in range(N_HEADS_B):
                v1[t, h, :HEAD_DIM, :] = vt[h * HEAD_DIM:(h + 1) * HEAD_DIM]
                v1[t, h, HEAD_DIM:, :] = ones
        knorm[...] = jnp.broadcast_to(jnp.sqrt(ksq_max), knorm.shape)

    spread = qnorm * knorm[...]
    safe = jnp.max(spread) * 2.0 + 4.0 < SAFE_SPREAD

    def scores(t, h):
        p = h // 2
        bias = bias_ref[jnp.minimum(i - t, bias_ref.shape[0] - 1)]
        return jnp.dot(k16[t, :, p * LANES:(p + 1) * LANES], qm[h],
                       preferred_element_type=F32) + bias

    @pl.when(safe)
    def _():
        m_sc[...] = spread + float(np.log2(len(PATTERNS)))

        def accumulate(tiles):
            units = [(t, h) for h in range(N_HEADS_B) for t in tiles]
            sts = [scores(t, h) for t, h in units]
            pv = [0.0] * N_HEADS_B
            for u, (t, h) in enumerate(units):
                ref8 = jnp.broadcast_to(m_sc[h], (8, TQ))
                if u + 1 < len(units):
                    ref8 = jnp.maximum(ref8, jnp.minimum(sts[u + 1][:8], NEG))
                pt = jnp.exp2((sts[u].reshape(TQ // 8, 8, TQ) - ref8[None]).reshape(TQ, TQ))
                pv[h] = pv[h] + jnp.dot(v1[t, h], pt.astype(BF16), preferred_element_type=F32)
            for h in range(N_HEADS_B):
                acc_sc[h] += pv[h]

        def pair_body(j, carry):
            accumulate([2 * j, 2 * j + 1])
            return carry

        lax.fori_loop(0, (i + 1) // 2, pair_body, 0)

        @pl.when(i % 2 == 0)
        def _():
            accumulate([i])

    @pl.when(jnp.logical_not(safe))
    def _():
        m_sc[...] = jnp.full(m_sc.shape, NEG, F32)

        def body(t, carry):
            sts = [scores(t, h) for h in range(N_HEADS_B)]
            tops = [jnp.max(st, axis=0, keepdims=True) for st in sts]
            for h in range(N_HEADS_B):
                floor = functools.reduce(jnp.minimum, tops[h:h + 1 + LOOKAHEAD],
                                         jnp.full_like(tops[0], NEG))
                m_old = m_sc[h]
                m_new = jnp.maximum(jnp.maximum(m_old, tops[h]), floor)
                alpha = jnp.exp2(m_old - m_new)
                pt = jnp.exp2(sts[h] - m_new).astype(BF16)
                acc_sc[h] = alpha * acc_sc[h] + jnp.dot(v1[t, h], pt,
                                                        preferred_element_type=F32)
                m_sc[h] = m_new
            return carry

        lax.fori_loop(0, i + 1, body, 0)

    _sample_values(*sample_p, svn_ref, cvt_ref, szb_ref, gob_ref, sb_ref)
    for p in range(D_B // LANES):
        halves = []
        for h in (2 * p, 2 * p + 1):
            acc = acc_sc[h]
            o = acc[:HEAD_DIM] / acc[HEAD_DIM:HEAD_DIM + 1]
            ms = jnp.sum(o * o, axis=0, keepdims=True) * (1.0 / HEAD_DIM)
            halves.append(o * lax.rsqrt(ms + EPS) * gobcol_ref[h * HEAD_DIM:(h + 1) * HEAD_DIM])
        sl = slice(p * LANES, (p + 1) * LANES)
        pair = jnp.concatenate(halves, axis=0).T
        b_ref[0, :, sl] = (pair * _silu(zb_ref[0, :, sl].astype(F32))).astype(b_ref.dtype)


def _attn_call(qt, kt, vt, zb, sq, sk_new, sv_new, cache_kt, cache_vt, szb, gob):
    bsz, _, s = qt.shape
    dbs, t, _ = sq.shape
    wb = cache_kt.shape[2]
    nt = s // TQ
    assert dbs == bsz * nt, "one sample sequence per prompt query tile"
    near = max(w for w, _ in PATTERNS[:-1])
    nkd = (near + TQ - 1) // TQ + 2
    qk = np.arange(TQ)[None, :] - np.arange(TQ)[:, None]
    bias = jnp.asarray(np.stack([_log2_bias(kd * TQ + qk) for kd in range(nkd)]))
    tq = np.tile(np.arange(t), N_HEADS_B)[:, None]
    bias_c = jnp.asarray(_log2_bias(wb + tq - np.arange(wb)[None, :]))
    d_new = np.where(np.arange(NEW_PAD)[None, :] < t, tq - np.arange(NEW_PAD)[None, :], -1)
    bias_n = jnp.asarray(_log2_bias(d_new))

    qtile = pl.BlockSpec((1, D_B, TQ), lambda b, i: (b, 0, i))
    tile = pl.BlockSpec((1, TQ, D_B), lambda b, i: (b, i, 0))
    whole = pl.BlockSpec((1, D_B, s), lambda b, i: (b, 0, 0))
    tok = pl.BlockSpec((1, t, D_B), lambda b, i: (b * nt + i, 0, 0))
    win = pl.BlockSpec((1, D_B, wb), lambda b, i: (b * nt + i, 0, 0))
    return pl.pallas_call(
        _attn_kernel,
        grid=(bsz, nt),
        in_specs=[qtile, whole, whole, _full(bias.shape), tile, _full((D_B, 1)),
                  tok, tok, tok, win, win, _full(bias_c.shape), _full(bias_n.shape), tok,
                  _full((1, D_B))],
        out_specs=[tile, tok],
        out_shape=[jax.ShapeDtypeStruct((bsz, s, D_B), BF16),
                   jax.ShapeDtypeStruct((dbs, t, D_B), F32)],
        scratch_shapes=[pltpu.VMEM((nt, TQ, D_B), BF16),
                        pltpu.VMEM((nt, N_HEADS_B, V_ROWS, TQ), BF16),
                        pltpu.VMEM((N_HEADS_B, LANES, TQ), BF16),
                        pltpu.VMEM((N_HEADS_B, 1, TQ), F32),
                        pltpu.VMEM((N_HEADS_B, 1, TQ), F32),
                        pltpu.VMEM((N_HEADS_B, V_ROWS, TQ), F32)],
        compiler_params=pltpu.CompilerParams(dimension_semantics=("parallel", "arbitrary"),
                                             vmem_limit_bytes=VMEM_LIMIT_ATTN),
        name="attn",
    )(qt, kt, vt, bias, zb, gob.reshape(D_B, 1),
      sq, sk_new, sv_new, cache_kt, cache_vt, bias_c, bias_n, szb, gob)


def _finish_kernel(x_ref, a_ref, b_ref, p_ref, wout_ref, wple_ref, wgate_ref, gple_ref, y_ref):
    for r0 in range(0, x_ref.shape[0], FIN_SUB):
        rows = slice(r0, r0 + FIN_SUB)
        h = (x_ref[rows, :]
             + jnp.dot(a_ref[rows, :].astype(BF16), wout_ref[:D_A, :], preferred_element_type=F32)
             + jnp.dot(b_ref[rows, :].astype(BF16), wout_ref[D_A:, :], preferred_element_type=F32))
        e = jnp.dot(p_ref[rows, :].astype(BF16), wple_ref[...], preferred_element_type=F32)
        e = e * _rsqrt_mean(e, D_MODEL) * gple_ref[...]
        h16 = h.astype(BF16)
        for c in range(0, D_MODEL, FIN_CHUNK):
            sl = slice(c, c + FIN_CHUNK)
            gate = jax.nn.sigmoid(jnp.dot(h16, wgate_ref[:, sl], preferred_element_type=F32))
            y_ref[rows, sl] = h[:, sl] + gate * e[:, sl]


def _finish_call(x, a, b, p, wout16, wple16, wgate16, gple):
    n = x.shape[0]
    tm = min(TM_FIN, n)
    row = lambda w: pl.BlockSpec((tm, w), lambda i: (i, 0))
    return pl.pallas_call(
        _finish_kernel,
        grid=(n // tm,),
        in_specs=[row(D_MODEL), row(D_A), row(D_B), row(D_PLE), _full(wout16.shape),
                  _full(wple16.shape), _full(wgate16.shape), _full((1, D_MODEL))],
        out_specs=row(D_MODEL),
        out_shape=jax.ShapeDtypeStruct((n, D_MODEL), F32),
        compiler_params=pltpu.CompilerParams(dimension_semantics=("parallel",),
                                             vmem_limit_bytes=VMEM_LIMIT),
        name="finish",
    )(x, a, b, p, wout16, wple16, wgate16, gple)


def _feature_major(c):
    b, p, h, dh = c.shape
    return jnp.transpose(c, (0, 2, 3, 1)).reshape(b, h * dh, p)


def _position_major(ct, h):
    b, hd, p = ct.shape
    return jnp.transpose(ct.reshape(b, h, hd // h, p), (0, 3, 1, 2))


def kernel(x_prompt, x_sample, cache_k, cache_v, p_prompt, p_sample, g_norm, w_in, w_s, b_s,
           g_va, g_oa, g_q, g_k, g_ob, w_out, w_ple, g_ple, w_ple_gate):
    depth = w_in.shape[0]
    assert depth == 1, "single-layer step"
    bsz, seq, _ = x_prompt.shape
    dbs, dseq, _ = x_sample.shape
    assert seq % TQ == 0 and seq <= max(w for w, _ in PATTERNS) and CHUNK % dseq == 0
    i = 0

    gn = g_norm[i].reshape(1, D_MODEL)
    win16 = w_in[i].astype(BF16)
    wqkvt16 = jnp.transpose(w_in[i][:, SEG_Q * SEG:(SEG_VB + 1) * SEG]).astype(BF16)
    gva = g_va[i].reshape(1, D_A)
    goa = g_oa[i].reshape(1, D_A)
    gq = jnp.tile(g_q[i], N_HEADS_B).reshape(1, D_B)
    gk = jnp.tile(g_k[i], N_HEADS_B).reshape(1, D_B)
    gob = g_ob[i].reshape(1, D_B)
    wout16 = w_out[i].astype(BF16)
    wple16 = w_ple[i].astype(BF16)
    wgate16 = w_ple_gate[i].astype(BF16)
    gple = g_ple[i].reshape(1, D_MODEL)

    a_p, qt_p, kt_p, vt_p, zb_p = _proj_call(
        x_prompt, gn, win16, wqkvt16, w_s[i], b_s[i][:, :, None], gva, goa, gq, gk,
        mix_block=CHUNK, q_dtype=BF16, feature_major=True, emit_vn=False)
    rep = CHUNK // dseq
    wm_s = jnp.tile(w_s[i][:, :dseq, :dseq], (1, rep, rep))
    bm_s = jnp.tile(b_s[i][:, :dseq], (1, rep))[:, :, None]
    xs = x_sample.reshape(1, dbs * dseq, D_MODEL)
    a_s, q_s, k_s, v_s, zb_s, vn_s = _proj_call(
        xs, gn, win16, wqkvt16, wm_s, bm_s, gva, goa, gq, gk,
        mix_block=dseq, q_dtype=F32, feature_major=False, emit_vn=True)

    tok = lambda z: z.reshape(dbs, dseq, D_B)
    b_p, b_s_ = _attn_call(qt_p, kt_p, vt_p, zb_p, tok(q_s), tok(k_s), tok(v_s),
                           _feature_major(cache_k[i]), _feature_major(cache_v[i]), tok(zb_s), gob)

    y_p = _finish_call(x_prompt.reshape(bsz * seq, D_MODEL), a_p.reshape(bsz * seq, D_A),
                       b_p.reshape(bsz * seq, D_B), p_prompt[i].reshape(bsz * seq, D_PLE),
                       wout16, wple16, wgate16, gple)
    y_s = _finish_call(xs.reshape(dbs * dseq, D_MODEL), a_s.reshape(dbs * dseq, D_A),
                       b_s_.reshape(dbs * dseq, D_B), p_sample[i].reshape(dbs * dseq, D_PLE),
                       wout16, wple16, wgate16, gple)

    hshape = (N_HEADS_B, HEAD_DIM)
    return (y_p.reshape(bsz, seq, D_MODEL),
            y_s.reshape(dbs, dseq, D_MODEL),
            _position_major(kt_p, N_HEADS_B)[None],
            _position_major(vt_p, N_HEADS_B)[None],
            k_s.reshape(1, dbs, dseq, *hshape),
            v_s.reshape(1, dbs, dseq, *hshape),
            vn_s.reshape(1, dbs, dseq, D_A))
```

```python
import functools

import numpy as np
import jax
import jax.numpy as jnp
from jax import lax
from jax.experimental import pallas as pl
from jax.experimental.pallas import tpu as pltpu

D_MODEL = 1024
D_A = 512
N_GROUPS_A = 4
GROUP_A = 128
CHUNK = 128
D_B = 512
N_HEADS_B = 8
HEAD_DIM = 64
PATTERNS = ((128, 1), (512, 4), (2048, 16))
D_PLE = 256
EPS = 1e-6
NEG = -1e30
SCALE = HEAD_DIM ** -0.5
Q_SCALE = SCALE * float(np.log2(np.e))
SEG = 512
SEG_U, SEG_VA, SEG_ZA, SEG_Q, SEG_K, SEG_VB, SEG_ZB = range(7)

LANES = 128
VMEM_LIMIT = 48 * 1024 * 1024
VMEM_LIMIT_ATTN = 56 * 1024 * 1024

TM_PROJ = 1024
PROJ_SUB = 256
TM_FIN = 1024
FIN_SUB = 256
FIN_CHUNK = 256
TQ = 256

BF16 = jnp.bfloat16
F32 = jnp.float32
NT = (((1,), (1,)), ((), ()))


def _multiplicity(d):
    c = np.zeros(d.shape, np.int32)
    for w, r in PATTERNS:
        c += ((d >= 0) & (d <= w) & (d % r == 0)).astype(np.int32)
    return c


def _log2_bias(d):
    c = _multiplicity(d)
    return np.where(c > 0, np.log2(np.maximum(c, 1).astype(np.float64)), NEG).astype(np.float32)


def _rsqrt_mean(x, n):
    return lax.rsqrt(jnp.sum(x * x, axis=-1, keepdims=True) * (1.0 / n) + EPS)


def _head_rms_scale(x):
    lane_lo = lax.broadcasted_iota(jnp.int32, x.shape, 1) < HEAD_DIM
    x2 = x * x
    lo = jnp.sum(jnp.where(lane_lo, x2, 0.0), axis=-1, keepdims=True)
    hi = jnp.sum(jnp.where(lane_lo, 0.0, x2), axis=-1, keepdims=True)
    ms = jnp.where(lane_lo, lo, hi) * (1.0 / HEAD_DIM)
    return lax.rsqrt(ms + EPS)


def _silu(z):
    return z * jax.nn.sigmoid(z)


def _full(shape):
    return pl.BlockSpec(shape, lambda *_: (0,) * len(shape))


def _head_norm_feature_major(xt, gcol_ref):
    m = xt.shape[1]
    x3 = xt.reshape(N_HEADS_B, HEAD_DIM, m)
    ms = jnp.sum(x3 * x3, axis=1, keepdims=True) * (1.0 / HEAD_DIM)
    g3 = gcol_ref[...].reshape(N_HEADS_B, HEAD_DIM, 1)
    return (x3 * lax.rsqrt(ms + EPS) * g3).reshape(D_B, m)


def _proj_kernel(*refs, mix_block, feature_major, emit_vn):
    refs = list(refs)
    x_ref, gn_ref, win_ref = refs[:3]
    del refs[:3]
    wqkv_ref = refs.pop(0) if feature_major else None
    (wm_ref, bm_ref, gva_ref, goa_ref, gq_ref, gk_ref, gqcol_ref, gkcol_ref,
     a_ref, q_ref, k_ref, v_ref, zb_ref) = refs[:13]
    del refs[:13]
    vn_ref = refs.pop(0) if emit_vn else None
    wqkvt = refs.pop(0) if feature_major else None

    if feature_major:
        @pl.when((pl.program_id(0) == 0) & (pl.program_id(1) == 0))
        def _():
            for c in range(0, wqkvt.shape[0], 2 * LANES):
                wqkvt[c:c + 2 * LANES, :] = wqkv_ref[:, c:c + 2 * LANES].T.astype(BF16)

    row = lax.broadcasted_iota(jnp.int32, (CHUNK, CHUNK), 0)
    col = lax.broadcasted_iota(jnp.int32, (CHUNK, CHUNK), 1)
    mix_mask = (row >= col) & ((row // mix_block) == (col // mix_block))
    wms = [jnp.where(mix_mask, wm_ref[g], 0.0).astype(BF16) for g in range(N_GROUPS_A)]

    for r0 in range(0, x_ref.shape[1], PROJ_SUB):
        rows = slice(r0, r0 + PROJ_SUB)
        x = x_ref[0, rows, :]
        h = (x * _rsqrt_mean(x, D_MODEL) * gn_ref[...]).astype(BF16)

        def seg(i):
            return jnp.dot(h, win_ref[:, i * SEG:(i + 1) * SEG], preferred_element_type=F32)

        zb_ref[0, rows, :] = seg(SEG_ZB).astype(zb_ref.dtype)
        if feature_major:
            qkvt = lax.dot_general(wqkvt[...], h, NT, preferred_element_type=F32)
            q_ref[0, :, rows] = (_head_norm_feature_major(qkvt[:D_B], gqcol_ref)
                                 * Q_SCALE).astype(q_ref.dtype)
            k_ref[0, :, rows] = _head_norm_feature_major(qkvt[D_B:2 * D_B], gkcol_ref)
            v_ref[0, :, rows] = qkvt[2 * D_B:]
        else:
            q, k = seg(SEG_Q), seg(SEG_K)
            v_ref[0, rows, :] = seg(SEG_VB)
            for j in range(D_B // LANES):
                sl = slice(j * LANES, (j + 1) * LANES)
                qs, ks = q[:, sl], k[:, sl]
                q_ref[0, rows, sl] = (qs * _head_rms_scale(qs) * gq_ref[:, sl]
                                      * Q_SCALE).astype(q_ref.dtype)
                k_ref[0, rows, sl] = ks * _head_rms_scale(ks) * gk_ref[:, sl]

        u, va, za = seg(SEG_U), seg(SEG_VA), seg(SEG_ZA)
        for g in range(N_GROUPS_A):
            sl = slice(g * GROUP_A, (g + 1) * GROUP_A)
            vs = va[:, sl]
            vn = vs * _rsqrt_mean(vs, GROUP_A) * gva_ref[:, sl]
            if emit_vn:
                vn_ref[0, rows, sl] = vn
            vn16 = vn.astype(BF16)
            mixed = jnp.concatenate(
                [jnp.dot(wms[g], vn16[c * CHUNK:(c + 1) * CHUNK], preferred_element_type=F32)
                 + bm_ref[g] for c in range(PROJ_SUB // CHUNK)], axis=0)
            a = u[:, sl] * mixed
            a = a * _rsqrt_mean(a, GROUP_A) * goa_ref[:, sl]
            a_ref[0, rows, sl] = (a * _silu(za[:, sl])).astype(a_ref.dtype)


def _proj_call(x, gn, win16, w_in, wm, bm, gva, goa, gq, gk, *, mix_block, q_dtype,
               feature_major, emit_vn):
    g, r, _ = x.shape
    tm = min(TM_PROJ, r)
    row = lambda w: pl.BlockSpec((1, tm, w), lambda b, i: (b, i, 0))
    n_qkv = SEG_VB + 1 - SEG_Q
    assert SEG_Q % n_qkv == 0, "q | k | v must be one aligned column block of w_in"
    if feature_major:
        qkv_shape, qkv_spec = (g, D_B, r), pl.BlockSpec((1, D_B, tm), lambda b, i: (b, 0, i))
        w_args = [win16, w_in]
        w_specs = [_full(win16.shape),
                   pl.BlockSpec((D_MODEL, n_qkv * SEG), lambda b, i: (0, SEG_Q // n_qkv))]
        scratch = [pltpu.VMEM((n_qkv * SEG, D_MODEL), BF16)]
    else:
        qkv_shape, qkv_spec = (g, r, D_B), row(D_B)
        w_args, w_specs, scratch = [win16], [_full(win16.shape)], []
    out_shape = [jax.ShapeDtypeStruct((g, r, D_A), BF16),
                 jax.ShapeDtypeStruct(qkv_shape, q_dtype),
                 jax.ShapeDtypeStruct(qkv_shape, F32),
                 jax.ShapeDtypeStruct(qkv_shape, F32),
                 jax.ShapeDtypeStruct((g, r, D_B), BF16)]
    out_specs = [row(D_A), qkv_spec, qkv_spec, qkv_spec, row(D_B)]
    if emit_vn:
        out_shape.append(jax.ShapeDtypeStruct((g, r, D_A), F32))
        out_specs.append(row(D_A))
    return pl.pallas_call(
        functools.partial(_proj_kernel, mix_block=mix_block, feature_major=feature_major,
                          emit_vn=emit_vn),
        grid=(g, r // tm),
        in_specs=[row(D_MODEL), _full((1, D_MODEL)), *w_specs,
                  _full(wm.shape), _full(bm.shape), _full((1, D_A)), _full((1, D_A)),
                  _full((1, D_B)), _full((1, D_B)), _full((D_B, 1)), _full((D_B, 1))],
        out_specs=out_specs,
        out_shape=out_shape,
        scratch_shapes=scratch,
        compiler_params=pltpu.CompilerParams(dimension_semantics=("arbitrary", "arbitrary"),
                                             vmem_limit_bytes=VMEM_LIMIT),
        name="proj",
    )(x, gn, *w_args, wm, bm, gva, goa, gq, gk, gq.reshape(D_B, 1), gk.reshape(D_B, 1))


def _gated_head_norm(o, zb, gob):
    return o * _head_rms_scale(o) * gob * _silu(zb)


V_ROWS = HEAD_DIM + 16
LOOKAHEAD = 1
NEW_PAD = LANES


def _sample_head_mask(t):
    r_i = lax.broadcasted_iota(jnp.int32, (N_HEADS_B * t, D_B), 0)
    c_i = lax.broadcasted_iota(jnp.int32, (N_HEADS_B * t, D_B), 1)
    return (r_i // t) == (c_i // HEAD_DIM)


def _sample_probs(q_ref, kn_ref, ckt_ref, bc_ref, bn_ref):
    t = q_ref.shape[1]
    q_rep = jnp.concatenate([q_ref[0]] * N_HEADS_B, axis=0)
    q_bd = jnp.where(_sample_head_mask(t), q_rep, 0.0).astype(BF16)
    pad = jnp.zeros((NEW_PAD - t, D_B), F32)
    k_new = jnp.concatenate([kn_ref[0], pad], axis=0).astype(BF16)

    s_c = jnp.dot(q_bd.astype(F32), ckt_ref[0], preferred_element_type=F32) + bc_ref[...]
    s_n = lax.dot_general(q_bd, k_new, NT, preferred_element_type=F32) + bn_ref[...]
    m = jnp.maximum(jnp.max(s_c, axis=-1, keepdims=True), jnp.max(s_n, axis=-1, keepdims=True))
    p_c = jnp.exp2(s_c - m)
    p_n = jnp.exp2(s_n - m)
    return p_c, p_n


def _sample_values(p_c, p_n, vn_ref, cvt_ref, zb_ref, gob_ref, b_ref):
    t = vn_ref.shape[1]
    pad = jnp.zeros((NEW_PAD - t, D_B), F32)
    v_new = jnp.concatenate([vn_ref[0], pad], axis=0).astype(BF16)
    l = jnp.sum(p_c, axis=-1, keepdims=True) + jnp.sum(p_n, axis=-1, keepdims=True)
    o = (lax.dot_general(p_c, cvt_ref[0], NT, preferred_element_type=F32)
         + jnp.dot(p_n.astype(BF16), v_new, preferred_element_type=F32)) / l
    o = jnp.where(_sample_head_mask(t), o, 0.0)
    o_tok = o[0:t]
    for hd in range(1, N_HEADS_B):
        o_tok = o_tok + o[hd * t:(hd + 1) * t]
    for p in range(D_B // LANES):
        sl = slice(p * LANES, (p + 1) * LANES)
        b_ref[0, :, sl] = _gated_head_norm(
            o_tok[:, sl], zb_ref[0, :, sl].astype(F32), gob_ref[:, sl]).astype(b_ref.dtype)


SAFE_SPREAD = 100.0


def _attn_kernel(qt_ref, kt_ref, vt_ref, bias_ref, zb_ref, gobcol_ref,
                 sq_ref, skn_ref, svn_ref, ckt_ref, cvt_ref, bc_ref, bn_ref, szb_ref, gob_ref,
                 b_ref, sb_ref, k16, v1, qm, knorm, m_sc, acc_sc):
    i = pl.program_id(1)
    sample_p = _sample_probs(sq_ref, skn_ref, ckt_ref, bc_ref, bn_ref)

    row_lo = lax.broadcasted_iota(jnp.int32, (LANES, TQ), 0) < HEAD_DIM
    for p in range(D_B // LANES):
        q2 = qt_ref[0, p * LANES:(p + 1) * LANES, :]
        qm[2 * p] = jnp.where(row_lo, q2, jnp.zeros_like(q2))
        qm[2 * p + 1] = jnp.where(row_lo, jnp.zeros_like(q2), q2)
    acc_sc[...] = jnp.zeros(acc_sc.shape, F32)
    qf = qt_ref[0].astype(F32).reshape(N_HEADS_B, HEAD_DIM, TQ)
    qnorm = jnp.sqrt(jnp.sum(qf * qf, axis=1, keepdims=True))

    @pl.when(i == 0)
    def _():
        ones = jnp.ones((V_ROWS - HEAD_DIM, TQ), BF16)
        ksq_max = jnp.zeros((N_HEADS_B, 1, 1), F32)
        for t in range(k16.shape[0]):
            cols = slice(t * TQ, (t + 1) * TQ)
            kt = kt_ref[0, :, cols]
            k16[t] = kt.T.astype(BF16)
            ksq = jnp.sum((kt * kt).reshape(N_HEADS_B, HEAD_DIM, TQ), axis=1, keepdims=True)
            ksq_max = jnp.maximum(ksq_max, jnp.max(ksq, axis=2, keepdims=True))
            vt = vt_ref[0, :, cols].astype(BF16)
            for h in range(N_HEADS_B):
                v1[t, h, :HEAD_DIM, :] = vt[h * HEAD_DIM:(h + 1) * HEAD_DIM]
                v1[t, h, HEAD_DIM:, :] = ones
        knorm[...] = jnp.broadcast_to(jnp.sqrt(ksq_max), knorm.shape)

    spread = qnorm * knorm[...]
    safe = jnp.max(spread) * 2.0 + 4.0 < SAFE_SPREAD

    def scores(t, h):
        p = h // 2
        bias = bias_ref[jnp.minimum(i - t, bias_ref.shape[0] - 1)]
        return jnp.dot(k16[t, :, p * LANES:(p + 1) * LANES], qm[h],
                       preferred_element_type=F32) + bias

    @pl.when(safe)
    def _():
        m_sc[...] = spread + float(np.log2(len(PATTERNS)))

        def accumulate(tiles):
            units = [(t, h) for h in range(N_HEADS_B) for t in tiles]
            sts = [scores(t, h) for t, h in units]
            pv = [0.0] * N_HEADS_B
            for u, (t, h) in enumerate(units):
                ref8 = jnp.broadcast_to(m_sc[h], (8, TQ))
                if u + 1 < len(units):
                    ref8 = jnp.maximum(ref8, jnp.minimum(sts[u + 1][:8], NEG))
                pt = jnp.exp2((sts[u].reshape(TQ // 8, 8, TQ) - ref8[None]).reshape(TQ, TQ))
                pv[h] = pv[h] + jnp.dot(v1[t, h], pt.astype(BF16), preferred_element_type=F32)
            for h in range(N_HEADS_B):
                acc_sc[h] += pv[h]

        def pair_body(j, carry):
            accumulate([2 * j, 2 * j + 1])
            return carry

        lax.fori_loop(0, (i + 1) // 2, pair_body, 0)

        @pl.when(i % 2 == 0)
        def _():
            accumulate([i])

    @pl.when(jnp.logical_not(safe))
    def _():
        m_sc[...] = jnp.full(m_sc.shape, NEG, F32)

        def body(t, carry):
            sts = [scores(t, h) for h in range(N_HEADS_B)]
            tops = [jnp.max(st, axis=0, keepdims=True) for st in sts]
            for h in range(N_HEADS_B):
                floor = functools.reduce(jnp.minimum, tops[h:h + 1 + LOOKAHEAD],
                                         jnp.full_like(tops[0], NEG))
                m_old = m_sc[h]
                m_new = jnp.maximum(jnp.maximum(m_old, tops[h]), floor)
                alpha = jnp.exp2(m_old - m_new)
                pt = jnp.exp2(sts[h] - m_new).astype(BF16)
                acc_sc[h] = alpha * acc_sc[h] + jnp.dot(v1[t, h], pt,
                                                        preferred_element_type=F32)
                m_sc[h] = m_new
            return carry

        lax.fori_loop(0, i + 1, body, 0)

    _sample_values(*sample_p, svn_ref, cvt_ref, szb_ref, gob_ref, sb_ref)
    for p in range(D_B // LANES):
        halves = []
        for h in (2 * p, 2 * p + 1):
            acc = acc_sc[h]
            o = acc[:HEAD_DIM] / acc[HEAD_DIM:HEAD_DIM + 1]
            ms = jnp.sum(o * o, axis=0, keepdims=True) * (1.0 / HEAD_DIM)
            halves.append(o * lax.rsqrt(ms + EPS) * gobcol_ref[h * HEAD_DIM:(h + 1) * HEAD_DIM])
        sl = slice(p * LANES, (p + 1) * LANES)
        pair = jnp.concatenate(halves, axis=0).T
        b_ref[0, :, sl] = (pair * _silu(zb_ref[0, :, sl].astype(F32))).astype(b_ref.dtype)


def _attn_call(qt, kt, vt, zb, sq, sk_new, sv_new, cache_kt, cache_vt, szb, gob):
    bsz, _, s = qt.shape
    dbs, t, _ = sq.shape
    wb = cache_kt.shape[2]
    nt = s // TQ
    assert dbs == bsz * nt, "one sample sequence per prompt query tile"
    near = max(w for w, _ in PATTERNS[:-1])
    nkd = (near + TQ - 1) // TQ + 2
    qk = np.arange(TQ)[None, :] - np.arange(TQ)[:, None]
    bias = jnp.asarray(np.stack([_log2_bias(kd * TQ + qk) for kd in range(nkd)]))
    tq = np.tile(np.arange(t), N_HEADS_B)[:, None]
    bias_c = jnp.asarray(_log2_bias(wb + tq - np.arange(wb)[None, :]))
    d_new = np.where(np.arange(NEW_PAD)[None, :] < t, tq - np.arange(NEW_PAD)[None, :], -1)
    bias_n = jnp.asarray(_log2_bias(d_new))

    qtile = pl.BlockSpec((1, D_B, TQ), lambda b, i: (b, 0, i))
    tile = pl.BlockSpec((1, TQ, D_B), lambda b, i: (b, i, 0))
    whole = pl.BlockSpec((1, D_B, s), lambda b, i: (b, 0, 0))
    tok = pl.BlockSpec((1, t, D_B), lambda b, i: (b * nt + i, 0, 0))
    win = pl.BlockSpec((1, D_B, wb), lambda b, i: (b * nt + i, 0, 0))
    return pl.pallas_call(
        _attn_kernel,
        grid=(bsz, nt),
        in_specs=[qtile, whole, whole, _full(bias.shape), tile, _full((D_B, 1)),
                  tok, tok, tok, win, win, _full(bias_c.shape), _full(bias_n.shape), tok,
                  _full((1, D_B))],
        out_specs=[tile, tok],
        out_shape=[jax.ShapeDtypeStruct((bsz, s, D_B), BF16),
                   jax.ShapeDtypeStruct((dbs, t, D_B), F32)],
        scratch_shapes=[pltpu.VMEM((nt, TQ, D_B), BF16),
                        pltpu.VMEM((nt, N_HEADS_B, V_ROWS, TQ), BF16),
                        pltpu.VMEM((N_HEADS_B, LANES, TQ), BF16),
                        pltpu.VMEM((N_HEADS_B, 1, TQ), F32),
                        pltpu.VMEM((N_HEADS_B, 1, TQ), F32),
                        pltpu.VMEM((N_HEADS_B, V_ROWS, TQ), F32)],
        compiler_params=pltpu.CompilerParams(dimension_semantics=("parallel", "arbitrary"),
                                             vmem_limit_bytes=VMEM_LIMIT_ATTN),
        name="attn",
    )(qt, kt, vt, bias, zb, gob.reshape(D_B, 1),
      sq, sk_new, sv_new, cache_kt, cache_vt, bias_c, bias_n, szb, gob)


def _finish_kernel(x_ref, a_ref, b_ref, p_ref, wout_ref, wple_ref, wgate_ref, gple_ref, y_ref):
    for r0 in range(0, x_ref.shape[0], FIN_SUB):
        rows = slice(r0, r0 + FIN_SUB)
        h = (x_ref[rows, :]
             + jnp.dot(a_ref[rows, :].astype(BF16), wout_ref[:D_A, :], preferred_element_type=F32)
             + jnp.dot(b_ref[rows, :].astype(BF16), wout_ref[D_A:, :], preferred_element_type=F32))
        e = jnp.dot(p_ref[rows, :].astype(BF16), wple_ref[...], preferred_element_type=F32)
        e = e * _rsqrt_mean(e, D_MODEL) * gple_ref[...]
        h16 = h.astype(BF16)
        for c in range(0, D_MODEL, FIN_CHUNK):
            sl = slice(c, c + FIN_CHUNK)
            gate = jax.nn.sigmoid(jnp.dot(h16, wgate_ref[:, sl], preferred_element_type=F32))
            y_ref[rows, sl] = h[:, sl] + gate * e[:, sl]


def _finish_call(x, a, b, p, wout16, wple16, wgate16, gple):
    n = x.shape[0]
    tm = min(TM_FIN, n)
    row = lambda w: pl.BlockSpec((tm, w), lambda i: (i, 0))
    return pl.pallas_call(
        _finish_kernel,
        grid=(n // tm,),
        in_specs=[row(D_MODEL), row(D_A), row(D_B), row(D_PLE), _full(wout16.shape),
                  _full(wple16.shape), _full(wgate16.shape), _full((1, D_MODEL))],
        out_specs=row(D_MODEL),
        out_shape=jax.ShapeDtypeStruct((n, D_MODEL), F32),
        compiler_params=pltpu.CompilerParams(dimension_semantics=("parallel",),
                                             vmem_limit_bytes=VMEM_LIMIT),
        name="finish",
    )(x, a, b, p, wout16, wple16, wgate16, gple)


def _feature_major(c):
    b, p, h, dh = c.shape
    return jnp.transpose(c, (0, 2, 3, 1)).reshape(b, h * dh, p)


def _position_major(ct, h):
    b, hd, p = ct.shape
    return jnp.transpose(ct.reshape(b, h, hd // h, p), (0, 3, 1, 2))


def kernel(x_prompt, x_sample, cache_k, cache_v, p_prompt, p_sample, g_norm, w_in, w_s, b_s,
           g_va, g_oa, g_q, g_k, g_ob, w_out, w_ple, g_ple, w_ple_gate):
    depth = w_in.shape[0]
    assert depth == 1, "single-layer step"
    bsz, seq, _ = x_prompt.shape
    dbs, dseq, _ = x_sample.shape
    assert seq % TQ == 0 and seq <= max(w for w, _ in PATTERNS) and CHUNK % dseq == 0
    i = 0

    gn = g_norm[i].reshape(1, D_MODEL)
    win16 = w_in[i].astype(BF16)
    gva = g_va[i].reshape(1, D_A)
    goa = g_oa[i].reshape(1, D_A)
    gq = jnp.tile(g_q[i], N_HEADS_B).reshape(1, D_B)
    gk = jnp.tile(g_k[i], N_HEADS_B).reshape(1, D_B)
    gob = g_ob[i].reshape(1, D_B)
    wout16 = w_out[i].astype(BF16)
    wple16 = w_ple[i].astype(BF16)
    wgate16 = w_ple_gate[i].astype(BF16)
    gple = g_ple[i].reshape(1, D_MODEL)

    a_p, qt_p, kt_p, vt_p, zb_p = _proj_call(
        x_prompt, gn, win16, w_in[i], w_s[i], b_s[i][:, :, None], gva, goa, gq, gk,
        mix_block=CHUNK, q_dtype=BF16, feature_major=True, emit_vn=False)
    rep = CHUNK // dseq
    wm_s = jnp.tile(w_s[i][:, :dseq, :dseq], (1, rep, rep))
    bm_s = jnp.tile(b_s[i][:, :dseq], (1, rep))[:, :, None]
    xs = x_sample.reshape(1, dbs * dseq, D_MODEL)
    a_s, q_s, k_s, v_s, zb_s, vn_s = _proj_call(
        xs, gn, win16, w_in[i], wm_s, bm_s, gva, goa, gq, gk,
        mix_block=dseq, q_dtype=F32, feature_major=False, emit_vn=True)

    tok = lambda z: z.reshape(dbs, dseq, D_B)
    b_p, b_s_ = _attn_call(qt_p, kt_p, vt_p, zb_p, tok(q_s), tok(k_s), tok(v_s),
                           _feature_major(cache_k[i]), _feature_major(cache_v[i]), tok(zb_s), gob)

    y_p = _finish_call(x_prompt.reshape(bsz * seq, D_MODEL), a_p.reshape(bsz * seq, D_A),
                       b_p.reshape(bsz * seq, D_B), p_prompt[i].reshape(bsz * seq, D_PLE),
                       wout16, wple16, wgate16, gple)
    y_s = _finish_call(xs.reshape(dbs * dseq, D_MODEL), a_s.reshape(dbs * dseq, D_A),
                       b_s_.reshape(dbs * dseq, D_B), p_sample[i].reshape(dbs * dseq, D_PLE),
                       wout16, wple16, wgate16, gple)

    hshape = (N_HEADS_B, HEAD_DIM)
    return (y_p.reshape(bsz, seq, D_MODEL),
            y_s.reshape(dbs, dseq, D_MODEL),
            _position_major(kt_p, N_HEADS_B)[None],
            _position_major(vt_p, N_HEADS_B)[None],
            k_s.reshape(1, dbs, dseq, *hshape),
            v_s.reshape(1, dbs, dseq, *hshape),
            vn_s.reshape(1, dbs, dseq, D_A))
```

```python
import functools

import numpy as np
import jax
import jax.numpy as jnp
from jax import lax
from jax.experimental import pallas as pl
from jax.experimental.pallas import tpu as pltpu

D_MODEL = 1024
D_A = 512
N_GROUPS_A = 4
GROUP_A = 128
CHUNK = 128
D_B = 512
N_HEADS_B = 8
HEAD_DIM = 64
PATTERNS = ((128, 1), (512, 4), (2048, 16))
D_PLE = 256
EPS = 1e-6
NEG = -1e30
SCALE = HEAD_DIM ** -0.5
Q_SCALE = SCALE * float(np.log2(np.e))
SEG = 512
SEG_U, SEG_VA, SEG_ZA, SEG_Q, SEG_K, SEG_VB, SEG_ZB = range(7)

LANES = 128
VMEM_LIMIT = 48 * 1024 * 1024
VMEM_LIMIT_ATTN = 56 * 1024 * 1024

TM_PROJ = 1024
PROJ_SUB = 256
TM_FIN = 1024
FIN_SUB = 256
FIN_CHUNK = 256
TQ = 256

BF16 = jnp.bfloat16
F32 = jnp.float32
NT = (((1,), (1,)), ((), ()))


def _multiplicity(d):
    c = np.zeros(d.shape, np.int32)
    for w, r in PATTERNS:
        c += ((d >= 0) & (d <= w) & (d % r == 0)).astype(np.int32)
    return c


def _log2_bias(d):
    c = _multiplicity(d)
    return np.where(c > 0, np.log2(np.maximum(c, 1).astype(np.float64)), NEG).astype(np.float32)


def _rsqrt_mean(x, n):
    return lax.rsqrt(jnp.sum(x * x, axis=-1, keepdims=True) * (1.0 / n) + EPS)


def _head_rms_scale(x):
    lane_lo = lax.broadcasted_iota(jnp.int32, x.shape, 1) < HEAD_DIM
    x2 = x * x
    lo = jnp.sum(jnp.where(lane_lo, x2, 0.0), axis=-1, keepdims=True)
    hi = jnp.sum(jnp.where(lane_lo, 0.0, x2), axis=-1, keepdims=True)
    ms = jnp.where(lane_lo, lo, hi) * (1.0 / HEAD_DIM)
    return lax.rsqrt(ms + EPS)


def _silu(z):
    return z * jax.nn.sigmoid(z)


def _full(shape):
    return pl.BlockSpec(shape, lambda *_: (0,) * len(shape))


def _head_norm_feature_major(xt, gcol_ref):
    m = xt.shape[1]
    x3 = xt.reshape(N_HEADS_B, HEAD_DIM, m)
    ms = jnp.sum(x3 * x3, axis=1, keepdims=True) * (1.0 / HEAD_DIM)
    g3 = gcol_ref[...].reshape(N_HEADS_B, HEAD_DIM, 1)
    return (x3 * lax.rsqrt(ms + EPS) * g3).reshape(D_B, m)


def _proj_kernel(*refs, mix_block, feature_major, emit_vn):
    refs = list(refs)
    x_ref, gn_ref, win_ref = refs[:3]
    del refs[:3]
    wqkv_ref = refs.pop(0) if feature_major else None
    (wm_ref, bm_ref, gva_ref, goa_ref, gq_ref, gk_ref, gqcol_ref, gkcol_ref,
     a_ref, q_ref, k_ref, v_ref, zb_ref) = refs[:13]
    del refs[:13]
    vn_ref = refs.pop(0) if emit_vn else None
    wqkvt = refs.pop(0) if feature_major else None

    if feature_major:
        @pl.when((pl.program_id(0) == 0) & (pl.program_id(1) == 0))
        def _():
            for c in range(0, wqkvt.shape[0], 2 * LANES):
                wqkvt[c:c + 2 * LANES, :] = wqkv_ref[:, c:c + 2 * LANES].T.astype(BF16)

    row = lax.broadcasted_iota(jnp.int32, (CHUNK, CHUNK), 0)
    col = lax.broadcasted_iota(jnp.int32, (CHUNK, CHUNK), 1)
    mix_mask = (row >= col) & ((row // mix_block) == (col // mix_block))
    wms = [jnp.where(mix_mask, wm_ref[g], 0.0).astype(BF16) for g in range(N_GROUPS_A)]

    for r0 in range(0, x_ref.shape[1], PROJ_SUB):
        rows = slice(r0, r0 + PROJ_SUB)
        x = x_ref[0, rows, :]
        h = (x * _rsqrt_mean(x, D_MODEL) * gn_ref[...]).astype(BF16)

        def seg(i):
            return jnp.dot(h, win_ref[:, i * SEG:(i + 1) * SEG], preferred_element_type=F32)

        zb_ref[0, rows, :] = seg(SEG_ZB).astype(zb_ref.dtype)
        if feature_major:
            qkvt = lax.dot_general(wqkvt[...], h, NT, preferred_element_type=F32)
            q_ref[0, :, rows] = (_head_norm_feature_major(qkvt[:D_B], gqcol_ref)
                                 * Q_SCALE).astype(q_ref.dtype)
            k_ref[0, :, rows] = _head_norm_feature_major(qkvt[D_B:2 * D_B], gkcol_ref)
            v_ref[0, :, rows] = qkvt[2 * D_B:]
        else:
            q, k = seg(SEG_Q), seg(SEG_K)
            v_ref[0, rows, :] = seg(SEG_VB)
            for j in range(D_B // LANES):
                sl = slice(j * LANES, (j + 1) * LANES)
                qs, ks = q[:, sl], k[:, sl]
                q_ref[0, rows, sl] = (qs * _head_rms_scale(qs) * gq_ref[:, sl]
                                      * Q_SCALE).astype(q_ref.dtype)
                k_ref[0, rows, sl] = ks * _head_rms_scale(ks) * gk_ref[:, sl]

        u, va, za = seg(SEG_U), seg(SEG_VA), seg(SEG_ZA)
        for g in range(N_GROUPS_A):
            sl = slice(g * GROUP_A, (g + 1) * GROUP_A)
            vs = va[:, sl]
            vn = vs * _rsqrt_mean(vs, GROUP_A) * gva_ref[:, sl]
            if emit_vn:
                vn_ref[0, rows, sl] = vn
            vn16 = vn.astype(BF16)
            mixed = jnp.concatenate(
                [jnp.dot(wms[g], vn16[c * CHUNK:(c + 1) * CHUNK], preferred_element_type=F32)
                 + bm_ref[g] for c in range(PROJ_SUB // CHUNK)], axis=0)
            a = u[:, sl] * mixed
            a = a * _rsqrt_mean(a, GROUP_A) * goa_ref[:, sl]
            a_ref[0, rows, sl] = (a * _silu(za[:, sl])).astype(a_ref.dtype)


def _proj_call(x, gn, win16, w_in, wm, bm, gva, goa, gq, gk, *, mix_block, q_dtype,
               feature_major, emit_vn):
    g, r, _ = x.shape
    tm = min(TM_PROJ, r)
    row = lambda w: pl.BlockSpec((1, tm, w), lambda b, i: (b, i, 0))
    n_qkv = SEG_VB + 1 - SEG_Q
    assert SEG_Q % n_qkv == 0, "q | k | v must be one aligned column block of w_in"
    if feature_major:
        qkv_shape, qkv_spec = (g, D_B, r), pl.BlockSpec((1, D_B, tm), lambda b, i: (b, 0, i))
        w_args = [win16, w_in]
        w_specs = [_full(win16.shape),
                   pl.BlockSpec((D_MODEL, n_qkv * SEG), lambda b, i: (0, SEG_Q // n_qkv))]
        scratch = [pltpu.VMEM((n_qkv * SEG, D_MODEL), BF16)]
    else:
        qkv_shape, qkv_spec = (g, r, D_B), row(D_B)
        w_args, w_specs, scratch = [win16], [_full(win16.shape)], []
    out_shape = [jax.ShapeDtypeStruct((g, r, D_A), BF16),
                 jax.ShapeDtypeStruct(qkv_shape, q_dtype),
                 jax.ShapeDtypeStruct(qkv_shape, F32),
                 jax.ShapeDtypeStruct(qkv_shape, F32),
                 jax.ShapeDtypeStruct((g, r, D_B), BF16)]
    out_specs = [row(D_A), qkv_spec, qkv_spec, qkv_spec, row(D_B)]
    if emit_vn:
        out_shape.append(jax.ShapeDtypeStruct((g, r, D_A), F32))
        out_specs.append(row(D_A))
    return pl.pallas_call(
        functools.partial(_proj_kernel, mix_block=mix_block, feature_major=feature_major,
                          emit_vn=emit_vn),
        grid=(g, r // tm),
        in_specs=[row(D_MODEL), _full((1, D_MODEL)), *w_specs,
                  _full(wm.shape), _full(bm.shape), _full((1, D_A)), _full((1, D_A)),
                  _full((1, D_B)), _full((1, D_B)), _full((D_B, 1)), _full((D_B, 1))],
        out_specs=out_specs,
        out_shape=out_shape,
        scratch_shapes=scratch,
        compiler_params=pltpu.CompilerParams(dimension_semantics=("arbitrary", "arbitrary"),
                                             vmem_limit_bytes=VMEM_LIMIT),
        name="proj",
    )(x, gn, *w_args, wm, bm, gva, goa, gq, gk, gq.reshape(D_B, 1), gk.reshape(D_B, 1))


def _gated_head_norm(o, zb, gob):
    return o * _head_rms_scale(o) * gob * _silu(zb)


V_ROWS = HEAD_DIM + 16
LOOKAHEAD = 1
NEW_PAD = LANES


def _sample_head_mask(t):
    r_i = lax.broadcasted_iota(jnp.int32, (N_HEADS_B * t, D_B), 0)
    c_i = lax.broadcasted_iota(jnp.int32, (N_HEADS_B * t, D_B), 1)
    return (r_i // t) == (c_i // HEAD_DIM)


def _sample_probs(q_ref, kn_ref, ckt_ref, bc_ref, bn_ref):
    t = q_ref.shape[1]
    q_rep = jnp.concatenate([q_ref[0]] * N_HEADS_B, axis=0)
    q_bd = jnp.where(_sample_head_mask(t), q_rep, 0.0).astype(BF16)
    pad = jnp.zeros((NEW_PAD - t, D_B), F32)
    k_new = jnp.concatenate([kn_ref[0], pad], axis=0).astype(BF16)

    s_c = jnp.dot(q_bd.astype(F32), ckt_ref[0], preferred_element_type=F32) + bc_ref[...]
    s_n = lax.dot_general(q_bd, k_new, NT, preferred_element_type=F32) + bn_ref[...]
    m = jnp.maximum(jnp.max(s_c, axis=-1, keepdims=True), jnp.max(s_n, axis=-1, keepdims=True))
    p_c = jnp.exp2(s_c - m)
    p_n = jnp.exp2(s_n - m)
    return p_c, p_n


def _sample_values(p_c, p_n, vn_ref, cvt_ref, zb_ref, gob_ref, b_ref):
    t = vn_ref.shape[1]
    pad = jnp.zeros((NEW_PAD - t, D_B), F32)
    v_new = jnp.concatenate([vn_ref[0], pad], axis=0).astype(BF16)
    l = jnp.sum(p_c, axis=-1, keepdims=True) + jnp.sum(p_n, axis=-1, keepdims=True)
    o = (lax.dot_general(p_c, cvt_ref[0], NT, preferred_element_type=F32)
         + jnp.dot(p_n.astype(BF16), v_new, preferred_element_type=F32)) / l
    o = jnp.where(_sample_head_mask(t), o, 0.0)
    o_tok = o[0:t]
    for hd in range(1, N_HEADS_B):
        o_tok = o_tok + o[hd * t:(hd + 1) * t]
    for p in range(D_B // LANES):
        sl = slice(p * LANES, (p + 1) * LANES)
        b_ref[0, :, sl] = _gated_head_norm(
            o_tok[:, sl], zb_ref[0, :, sl].astype(F32), gob_ref[:, sl]).astype(b_ref.dtype)


SAFE_SPREAD = 100.0


def _attn_kernel(qt_ref, kt_ref, vt_ref, bias_ref, zb_ref, gobcol_ref,
                 sq_ref, skn_ref, svn_ref, ckt_ref, cvt_ref, bc_ref, bn_ref, szb_ref, gob_ref,
                 b_ref, sb_ref, k16, v1, qm, knorm, m_sc, acc_sc):
    i = pl.program_id(1)

    @pl.when(i == 0)
    def _():
        ones = jnp.ones((V_ROWS - HEAD_DIM, TQ), BF16)
        ksq_max = jnp.zeros((N_HEADS_B, 1, 1), F32)
        for t in range(k16.shape[0]):
            cols = slice(t * TQ, (t + 1) * TQ)
            kt = kt_ref[0, :, cols]
            k16[t] = kt.T.astype(BF16)
            ksq = jnp.sum((kt * kt).reshape(N_HEADS_B, HEAD_DIM, TQ), axis=1, keepdims=True)
            ksq_max = jnp.maximum(ksq_max, jnp.max(ksq, axis=2, keepdims=True))
            vt = vt_ref[0, :, cols].astype(BF16)
            for h in range(N_HEADS_B):
                v1[t, h, :HEAD_DIM, :] = vt[h * HEAD_DIM:(h + 1) * HEAD_DIM]
                v1[t, h, HEAD_DIM:, :] = ones
        knorm[...] = jnp.broadcast_to(jnp.sqrt(ksq_max), knorm.shape)

    sample_p = _sample_probs(sq_ref, skn_ref, ckt_ref, bc_ref, bn_ref)

    row_lo = lax.broadcasted_iota(jnp.int32, (LANES, TQ), 0) < HEAD_DIM
    for p in range(D_B // LANES):
        q2 = qt_ref[0, p * LANES:(p + 1) * LANES, :]
        qm[2 * p] = jnp.where(row_lo, q2, jnp.zeros_like(q2))
        qm[2 * p + 1] = jnp.where(row_lo, jnp.zeros_like(q2), q2)
    acc_sc[...] = jnp.zeros(acc_sc.shape, F32)
    qf = qt_ref[0].astype(F32).reshape(N_HEADS_B, HEAD_DIM, TQ)
    qnorm = jnp.sqrt(jnp.sum(qf * qf, axis=1, keepdims=True))
    spread = qnorm * knorm[...]
    safe = jnp.max(spread) * 2.0 + 4.0 < SAFE_SPREAD

    def scores(t, h):
        p = h // 2
        bias = bias_ref[jnp.minimum(i - t, bias_ref.shape[0] - 1)]
        return jnp.dot(k16[t, :, p * LANES:(p + 1) * LANES], qm[h],
                       preferred_element_type=F32) + bias

    @pl.when(safe)
    def _():
        m_sc[...] = spread + float(np.log2(len(PATTERNS)))

        def accumulate(tiles):
            units = [(t, h) for h in range(N_HEADS_B) for t in tiles]
            sts = [scores(t, h) for t, h in units]
            pv = [0.0] * N_HEADS_B
            for u, (t, h) in enumerate(units):
                ref8 = jnp.broadcast_to(m_sc[h], (8, TQ))
                if u + 1 < len(units):
                    ref8 = jnp.maximum(ref8, jnp.minimum(sts[u + 1][:8], NEG))
                pt = jnp.exp2((sts[u].reshape(TQ // 8, 8, TQ) - ref8[None]).reshape(TQ, TQ))
                pv[h] = pv[h] + jnp.dot(v1[t, h], pt.astype(BF16), preferred_element_type=F32)
            for h in range(N_HEADS_B):
                acc_sc[h] += pv[h]

        def pair_body(j, carry):
            accumulate([2 * j, 2 * j + 1])
            return carry

        lax.fori_loop(0, (i + 1) // 2, pair_body, 0)

        @pl.when(i % 2 == 0)
        def _():
            accumulate([i])

    @pl.when(jnp.logical_not(safe))
    def _():
        m_sc[...] = jnp.full(m_sc.shape, NEG, F32)

        def body(t, carry):
            sts = [scores(t, h) for h in range(N_HEADS_B)]
            tops = [jnp.max(st, axis=0, keepdims=True) for st in sts]
            for h in range(N_HEADS_B):
                floor = functools.reduce(jnp.minimum, tops[h:h + 1 + LOOKAHEAD],
                                         jnp.full_like(tops[0], NEG))
                m_old = m_sc[h]
                m_new = jnp.maximum(jnp.maximum(m_old, tops[h]), floor)
                alpha = jnp.exp2(m_old - m_new)
                pt = jnp.exp2(sts[h] - m_new).astype(BF16)
                acc_sc[h] = alpha * acc_sc[h] + jnp.dot(v1[t, h], pt,
                                                        preferred_element_type=F32)
                m_sc[h] = m_new
            return carry

        lax.fori_loop(0, i + 1, body, 0)

    _sample_values(*sample_p, svn_ref, cvt_ref, szb_ref, gob_ref, sb_ref)
    for p in range(D_B // LANES):
        halves = []
        for h in (2 * p, 2 * p + 1):
            acc = acc_sc[h]
            o = acc[:HEAD_DIM] / acc[HEAD_DIM:HEAD_DIM + 1]
            ms = jnp.sum(o * o, axis=0, keepdims=True) * (1.0 / HEAD_DIM)
            halves.append(o * lax.rsqrt(ms + EPS) * gobcol_ref[h * HEAD_DIM:(h + 1) * HEAD_DIM])
        sl = slice(p * LANES, (p + 1) * LANES)
        pair = jnp.concatenate(halves, axis=0).T
        b_ref[0, :, sl] = (pair * _silu(zb_ref[0, :, sl].astype(F32))).astype(b_ref.dtype)


def _attn_call(qt, kt, vt, zb, sq, sk_new, sv_new, cache_kt, cache_vt, szb, gob):
    bsz, _, s = qt.shape
    dbs, t, _ = sq.shape
    wb = cache_kt.shape[2]
    nt = s // TQ
    assert dbs == bsz * nt, "one sample sequence per prompt query tile"
    near = max(w for w, _ in PATTERNS[:-1])
    nkd = (near + TQ - 1) // TQ + 2
    qk = np.arange(TQ)[None, :] - np.arange(TQ)[:, None]
    bias = jnp.asarray(np.stack([_log2_bias(kd * TQ + qk) for kd in range(nkd)]))
    tq = np.tile(np.arange(t), N_HEADS_B)[:, None]
    bias_c = jnp.asarray(_log2_bias(wb + tq - np.arange(wb)[None, :]))
    d_new = np.where(np.arange(NEW_PAD)[None, :] < t, tq - np.arange(NEW_PAD)[None, :], -1)
    bias_n = jnp.asarray(_log2_bias(d_new))

    qtile = pl.BlockSpec((1, D_B, TQ), lambda b, i: (b, 0, i))
    tile = pl.BlockSpec((1, TQ, D_B), lambda b, i: (b, i, 0))
    whole = pl.BlockSpec((1, D_B, s), lambda b, i: (b, 0, 0))
    tok = pl.BlockSpec((1, t, D_B), lambda b, i: (b * nt + i, 0, 0))
    win = pl.BlockSpec((1, D_B, wb), lambda b, i: (b * nt + i, 0, 0))
    return pl.pallas_call(
        _attn_kernel,
        grid=(bsz, nt),
        in_specs=[qtile, whole, whole, _full(bias.shape), tile, _full((D_B, 1)),
                  tok, tok, tok, win, win, _full(bias_c.shape), _full(bias_n.shape), tok,
                  _full((1, D_B))],
        out_specs=[tile, tok],
        out_shape=[jax.ShapeDtypeStruct((bsz, s, D_B), BF16),
                   jax.ShapeDtypeStruct((dbs, t, D_B), F32)],
        scratch_shapes=[pltpu.VMEM((nt, TQ, D_B), BF16),
                        pltpu.VMEM((nt, N_HEADS_B, V_ROWS, TQ), BF16),
                        pltpu.VMEM((N_HEADS_B, LANES, TQ), BF16),
                        pltpu.VMEM((N_HEADS_B, 1, TQ), F32),
                        pltpu.VMEM((N_HEADS_B, 1, TQ), F32),
                        pltpu.VMEM((N_HEADS_B, V_ROWS, TQ), F32)],
        compiler_params=pltpu.CompilerParams(dimension_semantics=("parallel", "arbitrary"),
                                             vmem_limit_bytes=VMEM_LIMIT_ATTN),
        name="attn",
    )(qt, kt, vt, bias, zb, gob.reshape(D_B, 1),
      sq, sk_new, sv_new, cache_kt, cache_vt, bias_c, bias_n, szb, gob)


def _finish_kernel(x_ref, a_ref, b_ref, p_ref, wout_ref, wple_ref, wgate_ref, gple_ref, y_ref):
    for r0 in range(0, x_ref.shape[0], FIN_SUB):
        rows = slice(r0, r0 + FIN_SUB)
        h = (x_ref[rows, :]
             + jnp.dot(a_ref[rows, :].astype(BF16), wout_ref[:D_A, :], preferred_element_type=F32)
             + jnp.dot(b_ref[rows, :].astype(BF16), wout_ref[D_A:, :], preferred_element_type=F32))
        e = jnp.dot(p_ref[rows, :].astype(BF16), wple_ref[...], preferred_element_type=F32)
        e = e * _rsqrt_mean(e, D_MODEL) * gple_ref[...]
        h16 = h.astype(BF16)
        for c in range(0, D_MODEL, FIN_CHUNK):
            sl = slice(c, c + FIN_CHUNK)
            gate = jax.nn.sigmoid(jnp.dot(h16, wgate_ref[:, sl], preferred_element_type=F32))
            y_ref[rows, sl] = h[:, sl] + gate * e[:, sl]


def _finish_call(x, a, b, p, wout16, wple16, wgate16, gple):
    n = x.shape[0]
    tm = min(TM_FIN, n)
    row = lambda w: pl.BlockSpec((tm, w), lambda i: (i, 0))
    return pl.pallas_call(
        _finish_kernel,
        grid=(n // tm,),
        in_specs=[row(D_MODEL), row(D_A), row(D_B), row(D_PLE), _full(wout16.shape),
                  _full(wple16.shape), _full(wgate16.shape), _full((1, D_MODEL))],
        out_specs=row(D_MODEL),
        out_shape=jax.ShapeDtypeStruct((n, D_MODEL), F32),
        compiler_params=pltpu.CompilerParams(dimension_semantics=("parallel",),
                                             vmem_limit_bytes=VMEM_LIMIT),
        name="finish",
    )(x, a, b, p, wout16, wple16, wgate16, gple)


def _feature_major(c):
    b, p, h, dh = c.shape
    return jnp.transpose(c, (0, 2, 3, 1)).reshape(b, h * dh, p)


def _position_major(ct, h):
    b, hd, p = ct.shape
    return jnp.transpose(ct.reshape(b, h, hd // h, p), (0, 3, 1, 2))


def kernel(x_prompt, x_sample, cache_k, cache_v, p_prompt, p_sample, g_norm, w_in, w_s, b_s,
           g_va, g_oa, g_q, g_k, g_ob, w_out, w_ple, g_ple, w_ple_gate):
    depth = w_in.shape[0]
    assert depth == 1, "single-layer step"
    bsz, seq, _ = x_prompt.shape
    dbs, dseq, _ = x_sample.shape
    assert seq % TQ == 0 and seq <= max(w for w, _ in PATTERNS) and CHUNK % dseq == 0
    i = 0

    gn = g_norm[i].reshape(1, D_MODEL)
    win16 = w_in[i].astype(BF16)
    gva = g_va[i].reshape(1, D_A)
    goa = g_oa[i].reshape(1, D_A)
    gq = jnp.tile(g_q[i], N_HEADS_B).reshape(1, D_B)
    gk = jnp.tile(g_k[i], N_HEADS_B).reshape(1, D_B)
    gob = g_ob[i].reshape(1, D_B)
    wout16 = w_out[i].astype(BF16)
    wple16 = w_ple[i].astype(BF16)
    wgate16 = w_ple_gate[i].astype(BF16)
    gple = g_ple[i].reshape(1, D_MODEL)

    a_p, qt_p, kt_p, vt_p, zb_p = _proj_call(
        x_prompt, gn, win16, w_in[i], w_s[i], b_s[i][:, :, None], gva, goa, gq, gk,
        mix_block=CHUNK, q_dtype=BF16, feature_major=True, emit_vn=False)
    rep = CHUNK // dseq
    wm_s = jnp.tile(w_s[i][:, :dseq, :dseq], (1, rep, rep))
    bm_s = jnp.tile(b_s[i][:, :dseq], (1, rep))[:, :, None]
    xs = x_sample.reshape(1, dbs * dseq, D_MODEL)
    a_s, q_s, k_s, v_s, zb_s, vn_s = _proj_call(
        xs, gn, win16, w_in[i], wm_s, bm_s, gva, goa, gq, gk,
        mix_block=dseq, q_dtype=F32, feature_major=False, emit_vn=True)

    tok = lambda z: z.reshape(dbs, dseq, D_B)
    b_p, b_s_ = _attn_call(qt_p, kt_p, vt_p, zb_p, tok(q_s), tok(k_s), tok(v_s),
                           _feature_major(cache_k[i]), _feature_major(cache_v[i]), tok(zb_s), gob)

    y_p = _finish_call(x_prompt.reshape(bsz * seq, D_MODEL), a_p.reshape(bsz * seq, D_A),
                       b_p.reshape(bsz * seq, D_B), p_prompt[i].reshape(bsz * seq, D_PLE),
                       wout16, wple16, wgate16, gple)
    y_s = _finish_call(xs.reshape(dbs * dseq, D_MODEL), a_s.reshape(dbs * dseq, D_A),
                       b_s_.reshape(dbs * dseq, D_B), p_sample[i].reshape(dbs * dseq, D_PLE),
                       wout16, wple16, wgate16, gple)

    hshape = (N_HEADS_B, HEAD_DIM)
    return (y_p.reshape(bsz, seq, D_MODEL),
            y_s.reshape(dbs, dseq, D_MODEL),
            _position_major(kt_p, N_HEADS_B)[None],
            _position_major(vt_p, N_HEADS_B)[None],
            k_s.reshape(1, dbs, dseq, *hshape),
            v_s.reshape(1, dbs, dseq, *hshape),
            vn_s.reshape(1, dbs, dseq, D_A))
```

```python
import functools

import numpy as np
import jax
import jax.numpy as jnp
from jax import lax
from jax.experimental import pallas as pl
from jax.experimental.pallas import tpu as pltpu

D_MODEL = 1024
D_A = 512
N_GROUPS_A = 4
GROUP_A = 128
CHUNK = 128
D_B = 512
N_HEADS_B = 8
HEAD_DIM = 64
PATTERNS = ((128, 1), (512, 4), (2048, 16))
D_PLE = 256
EPS = 1e-6
NEG = -1e30
SCALE = HEAD_DIM ** -0.5
Q_SCALE = SCALE * float(np.log2(np.e))
SEG = 512
SEG_U, SEG_VA, SEG_ZA, SEG_Q, SEG_K, SEG_VB, SEG_ZB = range(7)

LANES = 128
VMEM_LIMIT = 48 * 1024 * 1024
VMEM_LIMIT_ATTN = 56 * 1024 * 1024

TM_PROJ = 1024
PROJ_SUB = 256
TM_FIN = 1024
FIN_SUB = 256
FIN_CHUNK = 256
TQ = 256
TK = 256

BF16 = jnp.bfloat16
F32 = jnp.float32
NT = (((1,), (1,)), ((), ()))


def _multiplicity(d):
    c = np.zeros(d.shape, np.int32)
    for w, r in PATTERNS:
        c += ((d >= 0) & (d <= w) & (d % r == 0)).astype(np.int32)
    return c


def _log2_bias(d):
    c = _multiplicity(d)
    return np.where(c > 0, np.log2(np.maximum(c, 1).astype(np.float64)), NEG).astype(np.float32)


def _rsqrt_mean(x, n):
    return lax.rsqrt(jnp.sum(x * x, axis=-1, keepdims=True) * (1.0 / n) + EPS)


def _head_rms_scale(x):
    lane_lo = lax.broadcasted_iota(jnp.int32, x.shape, 1) < HEAD_DIM
    x2 = x * x
    lo = jnp.sum(jnp.where(lane_lo, x2, 0.0), axis=-1, keepdims=True)
    hi = jnp.sum(jnp.where(lane_lo, 0.0, x2), axis=-1, keepdims=True)
    ms = jnp.where(lane_lo, lo, hi) * (1.0 / HEAD_DIM)
    return lax.rsqrt(ms + EPS)


def _silu(z):
    return z * jax.nn.sigmoid(z)


def _full(shape):
    return pl.BlockSpec(shape, lambda *_: (0,) * len(shape))


def _head_norm_feature_major(xt, gcol_ref):
    m = xt.shape[1]
    x3 = xt.reshape(N_HEADS_B, HEAD_DIM, m)
    ms = jnp.sum(x3 * x3, axis=1, keepdims=True) * (1.0 / HEAD_DIM)
    g3 = gcol_ref[...].reshape(N_HEADS_B, HEAD_DIM, 1)
    return (x3 * lax.rsqrt(ms + EPS) * g3).reshape(D_B, m)


def _proj_kernel(*refs, mix_block, feature_major, emit_vn):
    refs = list(refs)
    x_ref, gn_ref, win_ref = refs[:3]
    del refs[:3]
    wqkv_ref = refs.pop(0) if feature_major else None
    (wm_ref, bm_ref, gva_ref, goa_ref, gq_ref, gk_ref, gqcol_ref, gkcol_ref,
     a_ref, q_ref, k_ref, v_ref, zb_ref) = refs[:13]
    del refs[:13]
    vn_ref = refs.pop(0) if emit_vn else None
    wqkvt = refs.pop(0) if feature_major else None

    if feature_major:
        @pl.when((pl.program_id(0) == 0) & (pl.program_id(1) == 0))
        def _():
            for c in range(0, wqkvt.shape[0], 2 * LANES):
                wqkvt[c:c + 2 * LANES, :] = wqkv_ref[:, c:c + 2 * LANES].T.astype(BF16)

    row = lax.broadcasted_iota(jnp.int32, (CHUNK, CHUNK), 0)
    col = lax.broadcasted_iota(jnp.int32, (CHUNK, CHUNK), 1)
    mix_mask = (row >= col) & ((row // mix_block) == (col // mix_block))
    wms = [jnp.where(mix_mask, wm_ref[g], 0.0).astype(BF16) for g in range(N_GROUPS_A)]

    for r0 in range(0, x_ref.shape[1], PROJ_SUB):
        rows = slice(r0, r0 + PROJ_SUB)
        x = x_ref[0, rows, :]
        h = (x * _rsqrt_mean(x, D_MODEL) * gn_ref[...]).astype(BF16)

        def seg(i):
            return jnp.dot(h, win_ref[:, i * SEG:(i + 1) * SEG], preferred_element_type=F32)

        zb_ref[0, rows, :] = seg(SEG_ZB).astype(zb_ref.dtype)
        if feature_major:
            qkvt = lax.dot_general(wqkvt[...], h, NT, preferred_element_type=F32)
            q_ref[0, :, rows] = (_head_norm_feature_major(qkvt[:D_B], gqcol_ref)
                                 * Q_SCALE).astype(q_ref.dtype)
            k_ref[0, :, rows] = _head_norm_feature_major(qkvt[D_B:2 * D_B], gkcol_ref)
            v_ref[0, :, rows] = qkvt[2 * D_B:]
        else:
            q, k = seg(SEG_Q), seg(SEG_K)
            v_ref[0, rows, :] = seg(SEG_VB)
            for j in range(D_B // LANES):
                sl = slice(j * LANES, (j + 1) * LANES)
                qs, ks = q[:, sl], k[:, sl]
                q_ref[0, rows, sl] = (qs * _head_rms_scale(qs) * gq_ref[:, sl]
                                      * Q_SCALE).astype(q_ref.dtype)
                k_ref[0, rows, sl] = ks * _head_rms_scale(ks) * gk_ref[:, sl]

        u, va, za = seg(SEG_U), seg(SEG_VA), seg(SEG_ZA)
        for g in range(N_GROUPS_A):
            sl = slice(g * GROUP_A, (g + 1) * GROUP_A)
            vs = va[:, sl]
            vn = vs * _rsqrt_mean(vs, GROUP_A) * gva_ref[:, sl]
            if emit_vn:
                vn_ref[0, rows, sl] = vn
            vn16 = vn.astype(BF16)
            mixed = jnp.concatenate(
                [jnp.dot(wms[g], vn16[c * CHUNK:(c + 1) * CHUNK], preferred_element_type=F32)
                 + bm_ref[g] for c in range(PROJ_SUB // CHUNK)], axis=0)
            a = u[:, sl] * mixed
            a = a * _rsqrt_mean(a, GROUP_A) * goa_ref[:, sl]
            a_ref[0, rows, sl] = (a * _silu(za[:, sl])).astype(a_ref.dtype)


def _proj_call(x, gn, win16, w_in, wm, bm, gva, goa, gq, gk, *, mix_block, q_dtype,
               feature_major, emit_vn):
    g, r, _ = x.shape
    tm = min(TM_PROJ, r)
    row = lambda w: pl.BlockSpec((1, tm, w), lambda b, i: (b, i, 0))
    n_qkv = SEG_VB + 1 - SEG_Q
    assert SEG_Q % n_qkv == 0, "q | k | v must be one aligned column block of w_in"
    if feature_major:
        qkv_shape, qkv_spec = (g, D_B, r), pl.BlockSpec((1, D_B, tm), lambda b, i: (b, 0, i))
        w_args = [win16, w_in]
        w_specs = [_full(win16.shape),
                   pl.BlockSpec((D_MODEL, n_qkv * SEG), lambda b, i: (0, SEG_Q // n_qkv))]
        scratch = [pltpu.VMEM((n_qkv * SEG, D_MODEL), BF16)]
    else:
        qkv_shape, qkv_spec = (g, r, D_B), row(D_B)
        w_args, w_specs, scratch = [win16], [_full(win16.shape)], []
    out_shape = [jax.ShapeDtypeStruct((g, r, D_A), BF16),
                 jax.ShapeDtypeStruct(qkv_shape, q_dtype),
                 jax.ShapeDtypeStruct(qkv_shape, F32),
                 jax.ShapeDtypeStruct(qkv_shape, F32),
                 jax.ShapeDtypeStruct((g, r, D_B), BF16)]
    out_specs = [row(D_A), qkv_spec, qkv_spec, qkv_spec, row(D_B)]
    if emit_vn:
        out_shape.append(jax.ShapeDtypeStruct((g, r, D_A), F32))
        out_specs.append(row(D_A))
    return pl.pallas_call(
        functools.partial(_proj_kernel, mix_block=mix_block, feature_major=feature_major,
                          emit_vn=emit_vn),
        grid=(g, r // tm),
        in_specs=[row(D_MODEL), _full((1, D_MODEL)), *w_specs,
                  _full(wm.shape), _full(bm.shape), _full((1, D_A)), _full((1, D_A)),
                  _full((1, D_B)), _full((1, D_B)), _full((D_B, 1)), _full((D_B, 1))],
        out_specs=out_specs,
        out_shape=out_shape,
        scratch_shapes=scratch,
        compiler_params=pltpu.CompilerParams(dimension_semantics=("arbitrary", "arbitrary"),
                                             vmem_limit_bytes=VMEM_LIMIT),
        name="proj",
    )(x, gn, *w_args, wm, bm, gva, goa, gq, gk, gq.reshape(D_B, 1), gk.reshape(D_B, 1))


def _gated_head_norm(o, zb, gob):
    return o * _head_rms_scale(o) * gob * _silu(zb)


V_ROWS = HEAD_DIM + 16
LOOKAHEAD = 1
NEW_PAD = LANES


def _sample_head_mask(t):
    r_i = lax.broadcasted_iota(jnp.int32, (N_HEADS_B * t, D_B), 0)
    c_i = lax.broadcasted_iota(jnp.int32, (N_HEADS_B * t, D_B), 1)
    return (r_i // t) == (c_i // HEAD_DIM)


def _sample_probs(q_ref, kn_ref, ckt_ref, bc_ref, bn_ref):
    t = q_ref.shape[1]
    q_rep = jnp.concatenate([q_ref[0]] * N_HEADS_B, axis=0)
    q_bd = jnp.where(_sample_head_mask(t), q_rep, 0.0).astype(BF16)
    pad = jnp.zeros((NEW_PAD - t, D_B), F32)
    k_new = jnp.concatenate([kn_ref[0], pad], axis=0).astype(BF16)

    s_c = jnp.dot(q_bd.astype(F32), ckt_ref[0], preferred_element_type=F32) + bc_ref[...]
    s_n = lax.dot_general(q_bd, k_new, NT, preferred_element_type=F32) + bn_ref[...]
    m = jnp.maximum(jnp.max(s_c, axis=-1, keepdims=True), jnp.max(s_n, axis=-1, keepdims=True))
    p_c = jnp.exp2(s_c - m)
    p_n = jnp.exp2(s_n - m)
    return p_c, p_n


def _sample_values(p_c, p_n, vn_ref, cvt_ref, zb_ref, gob_ref, b_ref):
    t = vn_ref.shape[1]
    pad = jnp.zeros((NEW_PAD - t, D_B), F32)
    v_new = jnp.concatenate([vn_ref[0], pad], axis=0).astype(BF16)
    l = jnp.sum(p_c, axis=-1, keepdims=True) + jnp.sum(p_n, axis=-1, keepdims=True)
    o = (lax.dot_general(p_c, cvt_ref[0], NT, preferred_element_type=F32)
         + jnp.dot(p_n.astype(BF16), v_new, preferred_element_type=F32)) / l
    o = jnp.where(_sample_head_mask(t), o, 0.0)
    o_tok = o[0:t]
    for hd in range(1, N_HEADS_B):
        o_tok = o_tok + o[hd * t:(hd + 1) * t]
    for p in range(D_B // LANES):
        sl = slice(p * LANES, (p + 1) * LANES)
        b_ref[0, :, sl] = _gated_head_norm(
            o_tok[:, sl], zb_ref[0, :, sl].astype(F32), gob_ref[:, sl]).astype(b_ref.dtype)


SAFE_SPREAD = 100.0


def _attn_kernel(qt_ref, kt_ref, vt_ref, bias_ref, zb_ref, gobcol_ref,
                 sq_ref, skn_ref, svn_ref, ckt_ref, cvt_ref, bc_ref, bn_ref, szb_ref, gob_ref,
                 b_ref, sb_ref, k16, v1, qm, knorm, m_sc, acc_sc):
    i = pl.program_id(1)

    @pl.when(i == 0)
    def _():
        ones = jnp.ones((V_ROWS - HEAD_DIM, TK), BF16)
        ksq_max = jnp.zeros((N_HEADS_B, 1, 1), F32)
        for t in range(k16.shape[0]):
            cols = slice(t * TK, (t + 1) * TK)
            kt = kt_ref[0, :, cols]
            k16[t] = kt.T.astype(BF16)
            ksq = jnp.sum((kt * kt).reshape(N_HEADS_B, HEAD_DIM, TK), axis=1, keepdims=True)
            ksq_max = jnp.maximum(ksq_max, jnp.max(ksq, axis=2, keepdims=True))
            vt = vt_ref[0, :, cols].astype(BF16)
            for h in range(N_HEADS_B):
                v1[t, h, :HEAD_DIM, :] = vt[h * HEAD_DIM:(h + 1) * HEAD_DIM]
                v1[t, h, HEAD_DIM:, :] = ones
        knorm[...] = jnp.broadcast_to(jnp.sqrt(ksq_max), knorm.shape)

    sample_p = _sample_probs(sq_ref, skn_ref, ckt_ref, bc_ref, bn_ref)

    row_lo = lax.broadcasted_iota(jnp.int32, (LANES, TQ), 0) < HEAD_DIM
    for p in range(D_B // LANES):
        q2 = qt_ref[0, p * LANES:(p + 1) * LANES, :]
        qm[2 * p] = jnp.where(row_lo, q2, jnp.zeros_like(q2))
        qm[2 * p + 1] = jnp.where(row_lo, jnp.zeros_like(q2), q2)
    qf = qt_ref[0].astype(F32).reshape(N_HEADS_B, HEAD_DIM, TQ)
    qnorm = jnp.sqrt(jnp.sum(qf * qf, axis=1, keepdims=True))
    spread = qnorm * knorm[...]
    safe = jnp.max(spread) * 2.0 + 4.0 < SAFE_SPREAD
    per_block = TQ // TK

    def scores(t, h):
        p = h // 2
        bias = bias_ref[jnp.minimum(per_block * (i + 1) - 1 - t, bias_ref.shape[0] - 1)]
        return jnp.dot(k16[t, :, p * LANES:(p + 1) * LANES], qm[h],
                       preferred_element_type=F32) + bias

    def accumulate_fixed(tiles, first):
        units = [(t, h) for h in range(N_HEADS_B) for t in tiles]
        sts = [scores(t, h) for t, h in units]
        pv = [0.0] * N_HEADS_B
        for u, (t, h) in enumerate(units):
            ref8 = jnp.broadcast_to(m_sc[h], (8, TQ))
            if u + 1 < len(units):
                ref8 = jnp.maximum(ref8, jnp.minimum(sts[u + 1][:8], NEG))
            pt = jnp.exp2((sts[u].reshape(TK // 8, 8, TQ) - ref8[None]).reshape(TK, TQ))
            pv[h] = pv[h] + jnp.dot(v1[t, h], pt.astype(BF16), preferred_element_type=F32)
        for h in range(N_HEADS_B):
            acc_sc[h] = pv[h] if first else acc_sc[h] + pv[h]

    m_sc[...] = spread + float(np.log2(len(PATTERNS)))
    accumulate_fixed([per_block * i + n for n in range(per_block)], first=True)

    @pl.when(safe)
    def _():
        def trip(j, carry):
            accumulate_fixed([2 * per_block * j + n for n in range(2 * per_block)], first=False)
            return carry

        lax.fori_loop(0, i // 2, trip, 0)

        @pl.when(i % 2 == 1)
        def _():
            accumulate_fixed([per_block * (i - 1) + n for n in range(per_block)], first=False)

    @pl.when(jnp.logical_not(safe))
    def _():
        m_sc[...] = jnp.full(m_sc.shape, NEG, F32)
        acc_sc[...] = jnp.zeros(acc_sc.shape, F32)

        def body(t, carry):
            sts = [scores(t, h) for h in range(N_HEADS_B)]
            tops = [jnp.max(st, axis=0, keepdims=True) for st in sts]
            for h in range(N_HEADS_B):
                floor = functools.reduce(jnp.minimum, tops[h:h + 1 + LOOKAHEAD],
                                         jnp.full_like(tops[0], NEG))
                m_old = m_sc[h]
                m_new = jnp.maximum(jnp.maximum(m_old, tops[h]), floor)
                alpha = jnp.exp2(m_old - m_new)
                pt = jnp.exp2(sts[h] - m_new).astype(BF16)
                acc_sc[h] = alpha * acc_sc[h] + jnp.dot(v1[t, h], pt,
                                                        preferred_element_type=F32)
                m_sc[h] = m_new
            return carry

        lax.fori_loop(0, per_block * (i + 1), body, 0)

    _sample_values(*sample_p, svn_ref, cvt_ref, szb_ref, gob_ref, sb_ref)
    for p in range(D_B // LANES):
        halves = []
        for h in (2 * p, 2 * p + 1):
            acc = acc_sc[h]
            o = acc[:HEAD_DIM] / acc[HEAD_DIM:HEAD_DIM + 1]
            ms = jnp.sum(o * o, axis=0, keepdims=True) * (1.0 / HEAD_DIM)
            halves.append(o * lax.rsqrt(ms + EPS) * gobcol_ref[h * HEAD_DIM:(h + 1) * HEAD_DIM])
        sl = slice(p * LANES, (p + 1) * LANES)
        pair = jnp.concatenate(halves, axis=0).T
        b_ref[0, :, sl] = (pair * _silu(zb_ref[0, :, sl].astype(F32))).astype(b_ref.dtype)


def _attn_call(qt, kt, vt, zb, sq, sk_new, sv_new, cache_kt, cache_vt, szb, gob):
    bsz, _, s = qt.shape
    dbs, t, _ = sq.shape
    wb = cache_kt.shape[2]
    nt = s // TQ
    assert dbs == bsz * nt, "one sample sequence per prompt query tile"
    near = max(w for w, _ in PATTERNS[:-1])
    diag = TQ // TK - 1
    nkd = (near + TK - 1) // TK + 2 + diag
    qk = np.arange(TQ)[None, :] - np.arange(TK)[:, None]
    bias = jnp.asarray(np.stack([_log2_bias((kd - diag) * TK + qk) for kd in range(nkd)]))
    tq = np.tile(np.arange(t), N_HEADS_B)[:, None]
    bias_c = jnp.asarray(_log2_bias(wb + tq - np.arange(wb)[None, :]))
    d_new = np.where(np.arange(NEW_PAD)[None, :] < t, tq - np.arange(NEW_PAD)[None, :], -1)
    bias_n = jnp.asarray(_log2_bias(d_new))

    qtile = pl.BlockSpec((1, D_B, TQ), lambda b, i: (b, 0, i))
    tile = pl.BlockSpec((1, TQ, D_B), lambda b, i: (b, i, 0))
    whole = pl.BlockSpec((1, D_B, s), lambda b, i: (b, 0, 0))
    tok = pl.BlockSpec((1, t, D_B), lambda b, i: (b * nt + i, 0, 0))
    win = pl.BlockSpec((1, D_B, wb), lambda b, i: (b * nt + i, 0, 0))
    return pl.pallas_call(
        _attn_kernel,
        grid=(bsz, nt),
        in_specs=[qtile, whole, whole, _full(bias.shape), tile, _full((D_B, 1)),
                  tok, tok, tok, win, win, _full(bias_c.shape), _full(bias_n.shape), tok,
                  _full((1, D_B))],
        out_specs=[tile, tok],
        out_shape=[jax.ShapeDtypeStruct((bsz, s, D_B), BF16),
                   jax.ShapeDtypeStruct((dbs, t, D_B), F32)],
        scratch_shapes=[pltpu.VMEM((s // TK, TK, D_B), BF16),
                        pltpu.VMEM((s // TK, N_HEADS_B, V_ROWS, TK), BF16),
                        pltpu.VMEM((N_HEADS_B, LANES, TQ), BF16),
                        pltpu.VMEM((N_HEADS_B, 1, TQ), F32),
                        pltpu.VMEM((N_HEADS_B, 1, TQ), F32),
                        pltpu.VMEM((N_HEADS_B, V_ROWS, TQ), F32)],
        compiler_params=pltpu.CompilerParams(dimension_semantics=("parallel", "arbitrary"),
                                             vmem_limit_bytes=VMEM_LIMIT_ATTN),
        name="attn",
    )(qt, kt, vt, bias, zb, gob.reshape(D_B, 1),
      sq, sk_new, sv_new, cache_kt, cache_vt, bias_c, bias_n, szb, gob)


def _finish_kernel(x_ref, a_ref, b_ref, p_ref, wout_ref, wple_ref, wgate_ref, gple_ref, y_ref):
    for r0 in range(0, x_ref.shape[0], FIN_SUB):
        rows = slice(r0, r0 + FIN_SUB)
        h = (x_ref[rows, :]
             + jnp.dot(a_ref[rows, :].astype(BF16), wout_ref[:D_A, :], preferred_element_type=F32)
             + jnp.dot(b_ref[rows, :].astype(BF16), wout_ref[D_A:, :], preferred_element_type=F32))
        e = jnp.dot(p_ref[rows, :].astype(BF16), wple_ref[...], preferred_element_type=F32)
        e = e * _rsqrt_mean(e, D_MODEL) * gple_ref[...]
        h16 = h.astype(BF16)
        for c in range(0, D_MODEL, FIN_CHUNK):
            sl = slice(c, c + FIN_CHUNK)
            gate = jax.nn.sigmoid(jnp.dot(h16, wgate_ref[:, sl], preferred_element_type=F32))
            y_ref[rows, sl] = h[:, sl] + gate * e[:, sl]


def _finish_call(x, a, b, p, wout16, wple16, wgate16, gple):
    n = x.shape[0]
    tm = min(TM_FIN, n)
    row = lambda w: pl.BlockSpec((tm, w), lambda i: (i, 0))
    return pl.pallas_call(
        _finish_kernel,
        grid=(n // tm,),
        in_specs=[row(D_MODEL), row(D_A), row(D_B), row(D_PLE), _full(wout16.shape),
                  _full(wple16.shape), _full(wgate16.shape), _full((1, D_MODEL))],
        out_specs=row(D_MODEL),
        out_shape=jax.ShapeDtypeStruct((n, D_MODEL), F32),
        compiler_params=pltpu.CompilerParams(dimension_semantics=("parallel",),
                                             vmem_limit_bytes=VMEM_LIMIT),
        name="finish",
    )(x, a, b, p, wout16, wple16, wgate16, gple)


def _feature_major(c):
    b, p, h, dh = c.shape
    return jnp.transpose(c, (0, 2, 3, 1)).reshape(b, h * dh, p)


def _position_major(ct, h):
    b, hd, p = ct.shape
    return jnp.transpose(ct.reshape(b, h, hd // h, p), (0, 3, 1, 2))


def kernel(x_prompt, x_sample, cache_k, cache_v, p_prompt, p_sample, g_norm, w_in, w_s, b_s,
           g_va, g_oa, g_q, g_k, g_ob, w_out, w_ple, g_ple, w_ple_gate):
    depth = w_in.shape[0]
    assert depth == 1, "single-layer step"
    bsz, seq, _ = x_prompt.shape
    dbs, dseq, _ = x_sample.shape
    assert seq % TQ == 0 and seq <= max(w for w, _ in PATTERNS) and CHUNK % dseq == 0
    i = 0

    gn = g_norm[i].reshape(1, D_MODEL)
    win16 = w_in[i].astype(BF16)
    gva = g_va[i].reshape(1, D_A)
    goa = g_oa[i].reshape(1, D_A)
    gq = jnp.tile(g_q[i], N_HEADS_B).reshape(1, D_B)
    gk = jnp.tile(g_k[i], N_HEADS_B).reshape(1, D_B)
    gob = g_ob[i].reshape(1, D_B)
    wout16 = w_out[i].astype(BF16)
    wple16 = w_ple[i].astype(BF16)
    wgate16 = w_ple_gate[i].astype(BF16)
    gple = g_ple[i].reshape(1, D_MODEL)

    a_p, qt_p, kt_p, vt_p, zb_p = _proj_call(
        x_prompt, gn, win16, w_in[i], w_s[i], b_s[i][:, :, None], gva, goa, gq, gk,
        mix_block=CHUNK, q_dtype=BF16, feature_major=True, emit_vn=False)
    rep = CHUNK // dseq
    wm_s = jnp.tile(w_s[i][:, :dseq, :dseq], (1, rep, rep))
    bm_s = jnp.tile(b_s[i][:, :dseq], (1, rep))[:, :, None]
    xs = x_sample.reshape(1, dbs * dseq, D_MODEL)
    a_s, q_s, k_s, v_s, zb_s, vn_s = _proj_call(
        xs, gn, win16, w_in[i], wm_s, bm_s, gva, goa, gq, gk,
        mix_block=dseq, q_dtype=F32, feature_major=False, emit_vn=True)

    tok = lambda z: z.reshape(dbs, dseq, D_B)
    b_p, b_s_ = _attn_call(qt_p, kt_p, vt_p, zb_p, tok(q_s), tok(k_s), tok(v_s),
                           _feature_major(cache_k[i]), _feature_major(cache_v[i]), tok(zb_s), gob)

    y_p = _finish_call(x_prompt.reshape(bsz * seq, D_MODEL), a_p.reshape(bsz * seq, D_A),
                       b_p.reshape(bsz * seq, D_B), p_prompt[i].reshape(bsz * seq, D_PLE),
                       wout16, wple16, wgate16, gple)
    y_s = _finish_call(xs.reshape(dbs * dseq, D_MODEL), a_s.reshape(dbs * dseq, D_A),
                       b_s_.reshape(dbs * dseq, D_B), p_sample[i].reshape(dbs * dseq, D_PLE),
                       wout16, wple16, wgate16, gple)

    hshape = (N_HEADS_B, HEAD_DIM)
    return (y_p.reshape(bsz, seq, D_MODEL),
            y_s.reshape(dbs, dseq, D_MODEL),
            _position_major(kt_p, N_HEADS_B)[None],
            _position_major(vt_p, N_HEADS_B)[None],
            k_s.reshape(1, dbs, dseq, *hshape),
            v_s.reshape(1, dbs, dseq, *hshape),
            vn_s.reshape(1, dbs, dseq, D_A))
```

```python
import functools

import numpy as np
import jax
import jax.numpy as jnp
from jax import lax
from jax.experimental import pallas as pl
from jax.experimental.pallas import tpu as pltpu

D_MODEL = 1024
D_A = 512
N_GROUPS_A = 4
GROUP_A = 128
CHUNK = 128
D_B = 512
N_HEADS_B = 8
HEAD_DIM = 64
PATTERNS = ((128, 1), (512, 4), (2048, 16))
D_PLE = 256
EPS = 1e-6
NEG = -1e30
SCALE = HEAD_DIM ** -0.5
Q_SCALE = SCALE * float(np.log2(np.e))
SEG = 512
SEG_U, SEG_VA, SEG_ZA, SEG_Q, SEG_K, SEG_VB, SEG_ZB = range(7)

LANES = 128
VMEM_LIMIT = 48 * 1024 * 1024
VMEM_LIMIT_ATTN = 56 * 1024 * 1024

TM_PROJ = 1024
PROJ_SUB = 256
TM_FIN = 1024
FIN_SUB = 256
FIN_CHUNK = 256
TQ = 256
TK = 256

BF16 = jnp.bfloat16
F32 = jnp.float32
NT = (((1,), (1,)), ((), ()))


def _multiplicity(d):
    c = np.zeros(d.shape, np.int32)
    for w, r in PATTERNS:
        c += ((d >= 0) & (d <= w) & (d % r == 0)).astype(np.int32)
    return c


def _log2_bias(d):
    c = _multiplicity(d)
    return np.where(c > 0, np.log2(np.maximum(c, 1).astype(np.float64)), NEG).astype(np.float32)


def _rsqrt_mean(x, n):
    return lax.rsqrt(jnp.sum(x * x, axis=-1, keepdims=True) * (1.0 / n) + EPS)


def _head_rms_scale(x):
    lane_lo = lax.broadcasted_iota(jnp.int32, x.shape, 1) < HEAD_DIM
    x2 = x * x
    lo = jnp.sum(jnp.where(lane_lo, x2, 0.0), axis=-1, keepdims=True)
    hi = jnp.sum(jnp.where(lane_lo, 0.0, x2), axis=-1, keepdims=True)
    ms = jnp.where(lane_lo, lo, hi) * (1.0 / HEAD_DIM)
    return lax.rsqrt(ms + EPS)


def _silu(z):
    return z * jax.nn.sigmoid(z)


def _full(shape):
    return pl.BlockSpec(shape, lambda *_: (0,) * len(shape))


def _head_norm_feature_major(xt, gcol_ref):
    m = xt.shape[1]
    x3 = xt.reshape(N_HEADS_B, HEAD_DIM, m)
    ms = jnp.sum(x3 * x3, axis=1, keepdims=True) * (1.0 / HEAD_DIM)
    g3 = gcol_ref[...].reshape(N_HEADS_B, HEAD_DIM, 1)
    return (x3 * lax.rsqrt(ms + EPS) * g3).reshape(D_B, m)


def _proj_kernel(*refs, mix_block, feature_major, emit_vn):
    refs = list(refs)
    x_ref, gn_ref, win_ref = refs[:3]
    del refs[:3]
    wqkv_ref = refs.pop(0) if feature_major else None
    (wm_ref, bm_ref, gva_ref, goa_ref, gq_ref, gk_ref, gqcol_ref, gkcol_ref,
     a_ref, q_ref, k_ref, v_ref, zb_ref) = refs[:13]
    del refs[:13]
    vn_ref = refs.pop(0) if emit_vn else None
    wqkvt = refs.pop(0) if feature_major else None

    if feature_major:
        @pl.when((pl.program_id(0) == 0) & (pl.program_id(1) == 0))
        def _():
            for c in range(0, wqkvt.shape[0], 2 * LANES):
                wqkvt[c:c + 2 * LANES, :] = wqkv_ref[:, c:c + 2 * LANES].T.astype(BF16)

    row = lax.broadcasted_iota(jnp.int32, (CHUNK, CHUNK), 0)
    col = lax.broadcasted_iota(jnp.int32, (CHUNK, CHUNK), 1)
    mix_mask = (row >= col) & ((row // mix_block) == (col // mix_block))
    wms = [jnp.where(mix_mask, wm_ref[g], 0.0).astype(BF16) for g in range(N_GROUPS_A)]

    for r0 in range(0, x_ref.shape[1], PROJ_SUB):
        rows = slice(r0, r0 + PROJ_SUB)
        x = x_ref[0, rows, :]
        h = (x * _rsqrt_mean(x, D_MODEL) * gn_ref[...]).astype(BF16)

        def seg(i):
            return jnp.dot(h, win_ref[:, i * SEG:(i + 1) * SEG], preferred_element_type=F32)

        zb_ref[0, rows, :] = seg(SEG_ZB).astype(zb_ref.dtype)
        if feature_major:
            qkvt = lax.dot_general(wqkvt[...], h, NT, preferred_element_type=F32)
            q_ref[0, :, rows] = (_head_norm_feature_major(qkvt[:D_B], gqcol_ref)
                                 * Q_SCALE).astype(q_ref.dtype)
            k_ref[0, :, rows] = _head_norm_feature_major(qkvt[D_B:2 * D_B], gkcol_ref)
            v_ref[0, :, rows] = qkvt[2 * D_B:]
        else:
            q, k = seg(SEG_Q), seg(SEG_K)
            v_ref[0, rows, :] = seg(SEG_VB)
            for j in range(D_B // LANES):
                sl = slice(j * LANES, (j + 1) * LANES)
                qs, ks = q[:, sl], k[:, sl]
                q_ref[0, rows, sl] = (qs * _head_rms_scale(qs) * gq_ref[:, sl]
                                      * Q_SCALE).astype(q_ref.dtype)
                k_ref[0, rows, sl] = ks * _head_rms_scale(ks) * gk_ref[:, sl]

        u, va, za = seg(SEG_U), seg(SEG_VA), seg(SEG_ZA)
        for g in range(N_GROUPS_A):
            sl = slice(g * GROUP_A, (g + 1) * GROUP_A)
            vs = va[:, sl]
            vn = vs * _rsqrt_mean(vs, GROUP_A) * gva_ref[:, sl]
            if emit_vn:
                vn_ref[0, rows, sl] = vn
            vn16 = vn.astype(BF16)
            mixed = jnp.concatenate(
                [jnp.dot(wms[g], vn16[c * CHUNK:(c + 1) * CHUNK], preferred_element_type=F32)
                 + bm_ref[g] for c in range(PROJ_SUB // CHUNK)], axis=0)
            a = u[:, sl] * mixed
            a = a * _rsqrt_mean(a, GROUP_A) * goa_ref[:, sl]
            a_ref[0, rows, sl] = (a * _silu(za[:, sl])).astype(a_ref.dtype)


def _proj_call(x, gn, win16, w_in, wm, bm, gva, goa, gq, gk, *, mix_block, q_dtype,
               feature_major, emit_vn):
    g, r, _ = x.shape
    tm = min(TM_PROJ, r)
    row = lambda w: pl.BlockSpec((1, tm, w), lambda b, i: (b, i, 0))
    n_qkv = SEG_VB + 1 - SEG_Q
    assert SEG_Q % n_qkv == 0, "q | k | v must be one aligned column block of w_in"
    if feature_major:
        qkv_shape, qkv_spec = (g, D_B, r), pl.BlockSpec((1, D_B, tm), lambda b, i: (b, 0, i))
        w_args = [win16, w_in]
        w_specs = [_full(win16.shape),
                   pl.BlockSpec((D_MODEL, n_qkv * SEG), lambda b, i: (0, SEG_Q // n_qkv))]
        scratch = [pltpu.VMEM((n_qkv * SEG, D_MODEL), BF16)]
    else:
        qkv_shape, qkv_spec = (g, r, D_B), row(D_B)
        w_args, w_specs, scratch = [win16], [_full(win16.shape)], []
    out_shape = [jax.ShapeDtypeStruct((g, r, D_A), BF16),
                 jax.ShapeDtypeStruct(qkv_shape, q_dtype),
                 jax.ShapeDtypeStruct(qkv_shape, F32),
                 jax.ShapeDtypeStruct(qkv_shape, F32),
                 jax.ShapeDtypeStruct((g, r, D_B), BF16)]
    out_specs = [row(D_A), qkv_spec, qkv_spec, qkv_spec, row(D_B)]
    if emit_vn:
        out_shape.append(jax.ShapeDtypeStruct((g, r, D_A), F32))
        out_specs.append(row(D_A))
    return pl.pallas_call(
        functools.partial(_proj_kernel, mix_block=mix_block, feature_major=feature_major,
                          emit_vn=emit_vn),
        grid=(g, r // tm),
        in_specs=[row(D_MODEL), _full((1, D_MODEL)), *w_specs,
                  _full(wm.shape), _full(bm.shape), _full((1, D_A)), _full((1, D_A)),
                  _full((1, D_B)), _full((1, D_B)), _full((D_B, 1)), _full((D_B, 1))],
        out_specs=out_specs,
        out_shape=out_shape,
        scratch_shapes=scratch,
        compiler_params=pltpu.CompilerParams(dimension_semantics=("arbitrary", "arbitrary"),
                                             vmem_limit_bytes=VMEM_LIMIT),
        name="proj",
    )(x, gn, *w_args, wm, bm, gva, goa, gq, gk, gq.reshape(D_B, 1), gk.reshape(D_B, 1))


def _gated_head_norm(o, zb, gob):
    return o * _head_rms_scale(o) * gob * _silu(zb)


V_ROWS = HEAD_DIM + 16
LOOKAHEAD = 1
NEW_PAD = LANES


def _sample_head_mask(t):
    r_i = lax.broadcasted_iota(jnp.int32, (N_HEADS_B * t, D_B), 0)
    c_i = lax.broadcasted_iota(jnp.int32, (N_HEADS_B * t, D_B), 1)
    return (r_i // t) == (c_i // HEAD_DIM)


def _sample_probs(q_ref, kn_ref, ckt_ref, bc_ref, bn_ref):
    t = q_ref.shape[1]
    q_rep = jnp.concatenate([q_ref[0]] * N_HEADS_B, axis=0)
    q_bd = jnp.where(_sample_head_mask(t), q_rep, 0.0).astype(BF16)
    pad = jnp.zeros((NEW_PAD - t, D_B), F32)
    k_new = jnp.concatenate([kn_ref[0], pad], axis=0).astype(BF16)

    s_c = jnp.dot(q_bd.astype(F32), ckt_ref[0], preferred_element_type=F32) + bc_ref[...]
    s_n = lax.dot_general(q_bd, k_new, NT, preferred_element_type=F32) + bn_ref[...]
    m = jnp.maximum(jnp.max(s_c, axis=-1, keepdims=True), jnp.max(s_n, axis=-1, keepdims=True))
    p_c = jnp.exp2(s_c - m)
    p_n = jnp.exp2(s_n - m)
    return p_c, p_n


def _sample_values(p_c, p_n, vn_ref, cvt_ref, zb_ref, gob_ref, b_ref):
    t = vn_ref.shape[1]
    pad = jnp.zeros((NEW_PAD - t, D_B), F32)
    v_new = jnp.concatenate([vn_ref[0], pad], axis=0).astype(BF16)
    l = jnp.sum(p_c, axis=-1, keepdims=True) + jnp.sum(p_n, axis=-1, keepdims=True)
    o = (lax.dot_general(p_c, cvt_ref[0], NT, preferred_element_type=F32)
         + jnp.dot(p_n.astype(BF16), v_new, preferred_element_type=F32)) / l
    o = jnp.where(_sample_head_mask(t), o, 0.0)
    o_tok = o[0:t]
    for hd in range(1, N_HEADS_B):
        o_tok = o_tok + o[hd * t:(hd + 1) * t]
    for p in range(D_B // LANES):
        sl = slice(p * LANES, (p + 1) * LANES)
        b_ref[0, :, sl] = _gated_head_norm(
            o_tok[:, sl], zb_ref[0, :, sl].astype(F32), gob_ref[:, sl]).astype(b_ref.dtype)


SAFE_SPREAD = 100.0


def _attn_kernel(qt_ref, kt_ref, vt_ref, bias_ref, zb_ref, gobcol_ref,
                 sq_ref, skn_ref, svn_ref, ckt_ref, cvt_ref, bc_ref, bn_ref, szb_ref, gob_ref,
                 b_ref, sb_ref, k16, v1, qm, knorm, m_sc, acc_sc):
    i = pl.program_id(1)

    @pl.when(i == 0)
    def _():
        ones = jnp.ones((V_ROWS - HEAD_DIM, TK), BF16)
        ksq_max = jnp.zeros((N_HEADS_B, 1, 1), F32)
        for t in range(k16.shape[0]):
            cols = slice(t * TK, (t + 1) * TK)
            kt = kt_ref[0, :, cols]
            k16[t] = kt.T.astype(BF16)
            ksq = jnp.sum((kt * kt).reshape(N_HEADS_B, HEAD_DIM, TK), axis=1, keepdims=True)
            ksq_max = jnp.maximum(ksq_max, jnp.max(ksq, axis=2, keepdims=True))
            vt = vt_ref[0, :, cols].astype(BF16)
            for h in range(N_HEADS_B):
                v1[t, h, :HEAD_DIM, :] = vt[h * HEAD_DIM:(h + 1) * HEAD_DIM]
                v1[t, h, HEAD_DIM:, :] = ones
        knorm[...] = jnp.broadcast_to(jnp.sqrt(ksq_max), knorm.shape)

    sample_p = _sample_probs(sq_ref, skn_ref, ckt_ref, bc_ref, bn_ref)

    row_lo = lax.broadcasted_iota(jnp.int32, (LANES, TQ), 0) < HEAD_DIM
    for p in range(D_B // LANES):
        q2 = qt_ref[0, p * LANES:(p + 1) * LANES, :]
        qm[2 * p] = jnp.where(row_lo, q2, jnp.zeros_like(q2))
        qm[2 * p + 1] = jnp.where(row_lo, jnp.zeros_like(q2), q2)
    qf = qt_ref[0].astype(F32).reshape(N_HEADS_B, HEAD_DIM, TQ)
    qnorm = jnp.sqrt(jnp.sum(qf * qf, axis=1, keepdims=True))
    spread = qnorm * knorm[...]
    safe = jnp.max(spread) * 2.0 + 4.0 < SAFE_SPREAD
    per_block = TQ // TK

    def scores(t, h):
        p = h // 2
        bias = bias_ref[jnp.minimum(per_block * (i + 1) - 1 - t, bias_ref.shape[0] - 1)]
        return jnp.dot(k16[t, :, p * LANES:(p + 1) * LANES], qm[h],
                       preferred_element_type=F32) + bias

    def accumulate_fixed(tiles, first):
        units = [(t, h) for h in range(N_HEADS_B) for t in tiles]
        sts = [scores(t, h) for t, h in units]
        pv = [0.0] * N_HEADS_B
        for u, (t, h) in enumerate(units):
            ref8 = jnp.broadcast_to(m_sc[h], (8, TQ))
            if u + 1 < len(units):
                ref8 = jnp.maximum(ref8, jnp.minimum(sts[u + 1][:8], NEG))
            pt = jnp.exp2((sts[u].reshape(TK // 8, 8, TQ) - ref8[None]).reshape(TK, TQ))
            pv[h] = pv[h] + jnp.dot(v1[t, h], pt.astype(BF16), preferred_element_type=F32)
        for h in range(N_HEADS_B):
            acc_sc[h] = pv[h] if first else acc_sc[h] + pv[h]

    m_sc[...] = spread + float(np.log2(len(PATTERNS)))
    accumulate_fixed([per_block * i + n for n in range(per_block)], first=True)

    @pl.when(safe)
    def _():
        def trip(j, carry):
            accumulate_fixed([2 * per_block * j + n for n in range(2 * per_block)], first=False)
            return carry

        lax.fori_loop(0, i // 2, trip, 0)

        @pl.when(i % 2 == 1)
        def _():
            accumulate_fixed([per_block * (i - 1) + n for n in range(per_block)], first=False)

    @pl.when(jnp.logical_not(safe))
    def _():
        m_sc[...] = jnp.full(m_sc.shape, NEG, F32)
        acc_sc[...] = jnp.zeros(acc_sc.shape, F32)

        def body(t, carry):
            sts = [scores(t, h) for h in range(N_HEADS_B)]
            tops = [jnp.max(st, axis=0, keepdims=True) for st in sts]
            for h in range(N_HEADS_B):
                floor = functools.reduce(jnp.minimum, tops[h:h + 1 + LOOKAHEAD],
                                         jnp.full_like(tops[0], NEG))
                m_old = m_sc[h]
                m_new = jnp.maximum(jnp.maximum(m_old, tops[h]), floor)
                alpha = jnp.exp2(m_old - m_new)
                pt = jnp.exp2(sts[h] - m_new).astype(BF16)
                acc_sc[h] = alpha * acc_sc[h] + jnp.dot(v1[t, h], pt,
                                                        preferred_element_type=F32)
                m_sc[h] = m_new
            return carry

        lax.fori_loop(0, per_block * (i + 1), body, 0)

    _sample_values(*sample_p, svn_ref, cvt_ref, szb_ref, gob_ref, sb_ref)
    acc = acc_sc[...]
    o = acc[:, :HEAD_DIM] / acc[:, HEAD_DIM:HEAD_DIM + 1]
    ms = jnp.sum(o * o, axis=1, keepdims=True) * (1.0 / HEAD_DIM)
    on = o * lax.rsqrt(ms + EPS) * gobcol_ref[...].reshape(N_HEADS_B, HEAD_DIM, 1)
    b_ref[0] = (on.reshape(D_B, TQ).T * _silu(zb_ref[0].astype(F32))).astype(b_ref.dtype)


def _attn_call(qt, kt, vt, zb, sq, sk_new, sv_new, cache_kt, cache_vt, szb, gob):
    bsz, _, s = qt.shape
    dbs, t, _ = sq.shape
    wb = cache_kt.shape[2]
    nt = s // TQ
    assert dbs == bsz * nt, "one sample sequence per prompt query tile"
    near = max(w for w, _ in PATTERNS[:-1])
    diag = TQ // TK - 1
    nkd = (near + TK - 1) // TK + 2 + diag
    qk = np.arange(TQ)[None, :] - np.arange(TK)[:, None]
    bias = jnp.asarray(np.stack([_log2_bias((kd - diag) * TK + qk) for kd in range(nkd)]))
    tq = np.tile(np.arange(t), N_HEADS_B)[:, None]
    bias_c = jnp.asarray(_log2_bias(wb + tq - np.arange(wb)[None, :]))
    d_new = np.where(np.arange(NEW_PAD)[None, :] < t, tq - np.arange(NEW_PAD)[None, :], -1)
    bias_n = jnp.asarray(_log2_bias(d_new))

    qtile = pl.BlockSpec((1, D_B, TQ), lambda b, i: (b, 0, i))
    tile = pl.BlockSpec((1, TQ, D_B), lambda b, i: (b, i, 0))
    whole = pl.BlockSpec((1, D_B, s), lambda b, i: (b, 0, 0))
    tok = pl.BlockSpec((1, t, D_B), lambda b, i: (b * nt + i, 0, 0))
    win = pl.BlockSpec((1, D_B, wb), lambda b, i: (b * nt + i, 0, 0))
    return pl.pallas_call(
        _attn_kernel,
        grid=(bsz, nt),
        in_specs=[qtile, whole, whole, _full(bias.shape), tile, _full((D_B, 1)),
                  tok, tok, tok, win, win, _full(bias_c.shape), _full(bias_n.shape), tok,
                  _full((1, D_B))],
        out_specs=[tile, tok],
        out_shape=[jax.ShapeDtypeStruct((bsz, s, D_B), BF16),
                   jax.ShapeDtypeStruct((dbs, t, D_B), F32)],
        scratch_shapes=[pltpu.VMEM((s // TK, TK, D_B), BF16),
                        pltpu.VMEM((s // TK, N_HEADS_B, V_ROWS, TK), BF16),
                        pltpu.VMEM((N_HEADS_B, LANES, TQ), BF16),
                        pltpu.VMEM((N_HEADS_B, 1, TQ), F32),
                        pltpu.VMEM((N_HEADS_B, 1, TQ), F32),
                        pltpu.VMEM((N_HEADS_B, V_ROWS, TQ), F32)],
        compiler_params=pltpu.CompilerParams(dimension_semantics=("parallel", "arbitrary"),
                                             vmem_limit_bytes=VMEM_LIMIT_ATTN),
        name="attn",
    )(qt, kt, vt, bias, zb, gob.reshape(D_B, 1),
      sq, sk_new, sv_new, cache_kt, cache_vt, bias_c, bias_n, szb, gob)


def _finish_kernel(x_ref, a_ref, b_ref, p_ref, wout_ref, wple_ref, wgate_ref, gple_ref, y_ref):
    for r0 in range(0, x_ref.shape[0], FIN_SUB):
        rows = slice(r0, r0 + FIN_SUB)
        h = (x_ref[rows, :]
             + jnp.dot(a_ref[rows, :].astype(BF16), wout_ref[:D_A, :], preferred_element_type=F32)
             + jnp.dot(b_ref[rows, :].astype(BF16), wout_ref[D_A:, :], preferred_element_type=F32))
        e = jnp.dot(p_ref[rows, :].astype(BF16), wple_ref[...], preferred_element_type=F32)
        e = e * _rsqrt_mean(e, D_MODEL) * gple_ref[...]
        h16 = h.astype(BF16)
        for c in range(0, D_MODEL, FIN_CHUNK):
            sl = slice(c, c + FIN_CHUNK)
            gate = jax.nn.sigmoid(jnp.dot(h16, wgate_ref[:, sl], preferred_element_type=F32))
            y_ref[rows, sl] = h[:, sl] + gate * e[:, sl]


def _finish_call(x, a, b, p, wout16, wple16, wgate16, gple):
    n = x.shape[0]
    tm = min(TM_FIN, n)
    row = lambda w: pl.BlockSpec((tm, w), lambda i: (i, 0))
    return pl.pallas_call(
        _finish_kernel,
        grid=(n // tm,),
        in_specs=[row(D_MODEL), row(D_A), row(D_B), row(D_PLE), _full(wout16.shape),
                  _full(wple16.shape), _full(wgate16.shape), _full((1, D_MODEL))],
        out_specs=row(D_MODEL),
        out_shape=jax.ShapeDtypeStruct((n, D_MODEL), F32),
        compiler_params=pltpu.CompilerParams(dimension_semantics=("parallel",),
                                             vmem_limit_bytes=VMEM_LIMIT),
        name="finish",
    )(x, a, b, p, wout16, wple16, wgate16, gple)


def _feature_major(c):
    b, p, h, dh = c.shape
    return jnp.transpose(c, (0, 2, 3, 1)).reshape(b, h * dh, p)


def _position_major(ct, h):
    b, hd, p = ct.shape
    return jnp.transpose(ct.reshape(b, h, hd // h, p), (0, 3, 1, 2))


def kernel(x_prompt, x_sample, cache_k, cache_v, p_prompt, p_sample, g_norm, w_in, w_s, b_s,
           g_va, g_oa, g_q, g_k, g_ob, w_out, w_ple, g_ple, w_ple_gate):
    depth = w_in.shape[0]
    assert depth == 1, "single-layer step"
    bsz, seq, _ = x_prompt.shape
    dbs, dseq, _ = x_sample.shape
    assert seq % TQ == 0 and seq <= max(w for w, _ in PATTERNS) and CHUNK % dseq == 0
    i = 0

    gn = g_norm[i].reshape(1, D_MODEL)
    win16 = w_in[i].astype(BF16)
    gva = g_va[i].reshape(1, D_A)
    goa = g_oa[i].reshape(1, D_A)
    gq = jnp.tile(g_q[i], N_HEADS_B).reshape(1, D_B)
    gk = jnp.tile(g_k[i], N_HEADS_B).reshape(1, D_B)
    gob = g_ob[i].reshape(1, D_B)
    wout16 = w_out[i].astype(BF16)
    wple16 = w_ple[i].astype(BF16)
    wgate16 = w_ple_gate[i].astype(BF16)
    gple = g_ple[i].reshape(1, D_MODEL)

    a_p, qt_p, kt_p, vt_p, zb_p = _proj_call(
        x_prompt, gn, win16, w_in[i], w_s[i], b_s[i][:, :, None], gva, goa, gq, gk,
        mix_block=CHUNK, q_dtype=BF16, feature_major=True, emit_vn=False)
    rep = CHUNK // dseq
    wm_s = jnp.tile(w_s[i][:, :dseq, :dseq], (1, rep, rep))
    bm_s = jnp.tile(b_s[i][:, :dseq], (1, rep))[:, :, None]
    xs = x_sample.reshape(1, dbs * dseq, D_MODEL)
    a_s, q_s, k_s, v_s, zb_s, vn_s = _proj_call(
        xs, gn, win16, w_in[i], wm_s, bm_s, gva, goa, gq, gk,
        mix_block=dseq, q_dtype=F32, feature_major=False, emit_vn=True)

    tok = lambda z: z.reshape(dbs, dseq, D_B)
    b_p, b_s_ = _attn_call(qt_p, kt_p, vt_p, zb_p, tok(q_s), tok(k_s), tok(v_s),
                           _feature_major(cache_k[i]), _feature_major(cache_v[i]), tok(zb_s), gob)

    y_p = _finish_call(x_prompt.reshape(bsz * seq, D_MODEL), a_p.reshape(bsz * seq, D_A),
                       b_p.reshape(bsz * seq, D_B), p_prompt[i].reshape(bsz * seq, D_PLE),
                       wout16, wple16, wgate16, gple)
    y_s = _finish_call(xs.reshape(dbs * dseq, D_MODEL), a_s.reshape(dbs * dseq, D_A),
                       b_s_.reshape(dbs * dseq, D_B), p_sample[i].reshape(dbs * dseq, D_PLE),
                       wout16, wple16, wgate16, gple)

    hshape = (N_HEADS_B, HEAD_DIM)
    return (y_p.reshape(bsz, seq, D_MODEL),
            y_s.reshape(dbs, dseq, D_MODEL),
            _position_major(kt_p, N_HEADS_B)[None],
            _position_major(vt_p, N_HEADS_B)[None],
            k_s.reshape(1, dbs, dseq, *hshape),
            v_s.reshape(1, dbs, dseq, *hshape),
            vn_s.reshape(1, dbs, dseq, D_A))
```

```python
import functools

import numpy as np
import jax
import jax.numpy as jnp
from jax import lax
from jax.experimental import pallas as pl
from jax.experimental.pallas import tpu as pltpu

D_MODEL = 1024
D_A = 512
N_GROUPS_A = 4
GROUP_A = 128
CHUNK = 128
D_B = 512
N_HEADS_B = 8
HEAD_DIM = 64
PATTERNS = ((128, 1), (512, 4), (2048, 16))
D_PLE = 256
EPS = 1e-6
NEG = -1e30
SCALE = HEAD_DIM ** -0.5
Q_SCALE = SCALE * float(np.log2(np.e))
SEG = 512
SEG_U, SEG_VA, SEG_ZA, SEG_Q, SEG_K, SEG_VB, SEG_ZB = range(7)

LANES = 128
VMEM_LIMIT = 48 * 1024 * 1024
VMEM_LIMIT_ATTN = 56 * 1024 * 1024

TM_PROJ = 1024
PROJ_SUB = 256
TM_FIN = 1024
FIN_SUB = 256
FIN_CHUNK = 256
TQ = 256
TK = 256

BF16 = jnp.bfloat16
F32 = jnp.float32
NT = (((1,), (1,)), ((), ()))


def _multiplicity(d):
    c = np.zeros(d.shape, np.int32)
    for w, r in PATTERNS:
        c += ((d >= 0) & (d <= w) & (d % r == 0)).astype(np.int32)
    return c


def _log2_bias(d):
    c = _multiplicity(d)
    return np.where(c > 0, np.log2(np.maximum(c, 1).astype(np.float64)), NEG).astype(np.float32)


def _rsqrt_mean(x, n):
    return lax.rsqrt(jnp.sum(x * x, axis=-1, keepdims=True) * (1.0 / n) + EPS)


def _head_rms_scale(x):
    lane_lo = lax.broadcasted_iota(jnp.int32, x.shape, 1) < HEAD_DIM
    x2 = x * x
    lo = jnp.sum(jnp.where(lane_lo, x2, 0.0), axis=-1, keepdims=True)
    hi = jnp.sum(jnp.where(lane_lo, 0.0, x2), axis=-1, keepdims=True)
    ms = jnp.where(lane_lo, lo, hi) * (1.0 / HEAD_DIM)
    return lax.rsqrt(ms + EPS)


def _silu(z):
    return z * jax.nn.sigmoid(z)


def _full(shape):
    return pl.BlockSpec(shape, lambda *_: (0,) * len(shape))


def _head_norm_feature_major(xt, gcol_ref):
    m = xt.shape[1]
    x3 = xt.reshape(N_HEADS_B, HEAD_DIM, m)
    ms = jnp.sum(x3 * x3, axis=1, keepdims=True) * (1.0 / HEAD_DIM)
    g3 = gcol_ref[...].reshape(N_HEADS_B, HEAD_DIM, 1)
    return (x3 * lax.rsqrt(ms + EPS) * g3).reshape(D_B, m)


def _proj_kernel(*refs, mix_block, feature_major, emit_vn):
    refs = list(refs)
    x_ref, gn_ref, win_ref = refs[:3]
    del refs[:3]
    wqkv_ref = refs.pop(0) if feature_major else None
    (wm_ref, bm_ref, gva_ref, goa_ref, gq_ref, gk_ref, gqcol_ref, gkcol_ref,
     a_ref, q_ref, k_ref, v_ref, zb_ref) = refs[:13]
    del refs[:13]
    vn_ref = refs.pop(0) if emit_vn else None
    wqkvt = refs.pop(0) if feature_major else None

    if feature_major:
        @pl.when((pl.program_id(0) == 0) & (pl.program_id(1) == 0))
        def _():
            for c in range(0, wqkvt.shape[0], 2 * LANES):
                wqkvt[c:c + 2 * LANES, :] = wqkv_ref[:, c:c + 2 * LANES].T.astype(BF16)

    row = lax.broadcasted_iota(jnp.int32, (CHUNK, CHUNK), 0)
    col = lax.broadcasted_iota(jnp.int32, (CHUNK, CHUNK), 1)
    mix_mask = (row >= col) & ((row // mix_block) == (col // mix_block))
    wms = [jnp.where(mix_mask, wm_ref[g], 0.0).astype(BF16) for g in range(N_GROUPS_A)]

    for r0 in range(0, x_ref.shape[1], PROJ_SUB):
        rows = slice(r0, r0 + PROJ_SUB)
        x = x_ref[0, rows, :]
        h = (x * _rsqrt_mean(x, D_MODEL) * gn_ref[...]).astype(BF16)

        def seg(i):
            return jnp.dot(h, win_ref[:, i * SEG:(i + 1) * SEG], preferred_element_type=F32)

        zb_ref[0, rows, :] = seg(SEG_ZB).astype(zb_ref.dtype)
        if feature_major:
            qkvt = lax.dot_general(wqkvt[...], h, NT, preferred_element_type=F32)
            q_ref[0, :, rows] = (_head_norm_feature_major(qkvt[:D_B], gqcol_ref)
                                 * Q_SCALE).astype(q_ref.dtype)
            k_ref[0, :, rows] = _head_norm_feature_major(qkvt[D_B:2 * D_B], gkcol_ref)
            v_ref[0, :, rows] = qkvt[2 * D_B:]
        else:
            q, k = seg(SEG_Q), seg(SEG_K)
            v_ref[0, rows, :] = seg(SEG_VB)
            for j in range(D_B // LANES):
                sl = slice(j * LANES, (j + 1) * LANES)
                qs, ks = q[:, sl], k[:, sl]
                q_ref[0, rows, sl] = (qs * _head_rms_scale(qs) * gq_ref[:, sl]
                                      * Q_SCALE).astype(q_ref.dtype)
                k_ref[0, rows, sl] = ks * _head_rms_scale(ks) * gk_ref[:, sl]

        u, va, za = seg(SEG_U), seg(SEG_VA), seg(SEG_ZA)
        for g in range(N_GROUPS_A):
            sl = slice(g * GROUP_A, (g + 1) * GROUP_A)
            vs = va[:, sl]
            vn = vs * _rsqrt_mean(vs, GROUP_A) * gva_ref[:, sl]
            if emit_vn:
                vn_ref[0, rows, sl] = vn
            vn16 = vn.astype(BF16)
            mixed = jnp.concatenate(
                [jnp.dot(wms[g], vn16[c * CHUNK:(c + 1) * CHUNK], preferred_element_type=F32)
                 + bm_ref[g] for c in range(PROJ_SUB // CHUNK)], axis=0)
            a = u[:, sl] * mixed
            a = a * _rsqrt_mean(a, GROUP_A) * goa_ref[:, sl]
            a_ref[0, rows, sl] = (a * _silu(za[:, sl])).astype(a_ref.dtype)


def _proj_call(x, gn, win16, w_in, wm, bm, gva, goa, gq, gk, *, mix_block, q_dtype,
               feature_major, emit_vn):
    g, r, _ = x.shape
    tm = min(TM_PROJ, r)
    row = lambda w: pl.BlockSpec((1, tm, w), lambda b, i: (b, i, 0))
    n_qkv = SEG_VB + 1 - SEG_Q
    assert SEG_Q % n_qkv == 0, "q | k | v must be one aligned column block of w_in"
    if feature_major:
        qkv_shape, qkv_spec = (g, D_B, r), pl.BlockSpec((1, D_B, tm), lambda b, i: (b, 0, i))
        w_args = [win16, w_in]
        w_specs = [_full(win16.shape),
                   pl.BlockSpec((D_MODEL, n_qkv * SEG), lambda b, i: (0, SEG_Q // n_qkv))]
        scratch = [pltpu.VMEM((n_qkv * SEG, D_MODEL), BF16)]
    else:
        qkv_shape, qkv_spec = (g, r, D_B), row(D_B)
        w_args, w_specs, scratch = [win16], [_full(win16.shape)], []
    out_shape = [jax.ShapeDtypeStruct((g, r, D_A), BF16),
                 jax.ShapeDtypeStruct(qkv_shape, q_dtype),
                 jax.ShapeDtypeStruct(qkv_shape, F32),
                 jax.ShapeDtypeStruct(qkv_shape, F32),
                 jax.ShapeDtypeStruct((g, r, D_B), BF16)]
    out_specs = [row(D_A), qkv_spec, qkv_spec, qkv_spec, row(D_B)]
    if emit_vn:
        out_shape.append(jax.ShapeDtypeStruct((g, r, D_A), F32))
        out_specs.append(row(D_A))
    return pl.pallas_call(
        functools.partial(_proj_kernel, mix_block=mix_block, feature_major=feature_major,
                          emit_vn=emit_vn),
        grid=(g, r // tm),
        in_specs=[row(D_MODEL), _full((1, D_MODEL)), *w_specs,
                  _full(wm.shape), _full(bm.shape), _full((1, D_A)), _full((1, D_A)),
                  _full((1, D_B)), _full((1, D_B)), _full((D_B, 1)), _full((D_B, 1))],
        out_specs=out_specs,
        out_shape=out_shape,
        scratch_shapes=scratch,
        compiler_params=pltpu.CompilerParams(dimension_semantics=("arbitrary", "arbitrary"),
                                             vmem_limit_bytes=VMEM_LIMIT),
        name="proj",
    )(x, gn, *w_args, wm, bm, gva, goa, gq, gk, gq.reshape(D_B, 1), gk.reshape(D_B, 1))


def _gated_head_norm(o, zb, gob):
    return o * _head_rms_scale(o) * gob * _silu(zb)


V_ROWS = HEAD_DIM + 16
LOOKAHEAD = 1
NEW_PAD = LANES


def _sample_head_mask(t):
    r_i = lax.broadcasted_iota(jnp.int32, (N_HEADS_B * t, D_B), 0)
    c_i = lax.broadcasted_iota(jnp.int32, (N_HEADS_B * t, D_B), 1)
    return (r_i // t) == (c_i // HEAD_DIM)


def _sample_probs(q_ref, kn_ref, ckt_ref, bc_ref, bn_ref):
    t = q_ref.shape[1]
    q_rep = jnp.concatenate([q_ref[0]] * N_HEADS_B, axis=0)
    q_bd = jnp.where(_sample_head_mask(t), q_rep, 0.0).astype(BF16)
    pad = jnp.zeros((NEW_PAD - t, D_B), F32)
    k_new = jnp.concatenate([kn_ref[0], pad], axis=0).astype(BF16)

    s_c = jnp.dot(q_bd.astype(F32), ckt_ref[0], preferred_element_type=F32) + bc_ref[...]
    s_n = lax.dot_general(q_bd, k_new, NT, preferred_element_type=F32) + bn_ref[...]
    m = jnp.maximum(jnp.max(s_c, axis=-1, keepdims=True), jnp.max(s_n, axis=-1, keepdims=True))
    p_c = jnp.exp2(s_c - m)
    p_n = jnp.exp2(s_n - m)
    return p_c, p_n


def _sample_values(p_c, p_n, vn_ref, cvt_ref, zb_ref, gob_ref, b_ref):
    t = vn_ref.shape[1]
    pad = jnp.zeros((NEW_PAD - t, D_B), F32)
    v_new = jnp.concatenate([vn_ref[0], pad], axis=0).astype(BF16)
    l = jnp.sum(p_c, axis=-1, keepdims=True) + jnp.sum(p_n, axis=-1, keepdims=True)
    o = (lax.dot_general(p_c, cvt_ref[0], NT, preferred_element_type=F32)
         + jnp.dot(p_n.astype(BF16), v_new, preferred_element_type=F32)) / l
    o = jnp.where(_sample_head_mask(t), o, 0.0)
    o_tok = o[0:t]
    for hd in range(1, N_HEADS_B):
        o_tok = o_tok + o[hd * t:(hd + 1) * t]
    for p in range(D_B // LANES):
        sl = slice(p * LANES, (p + 1) * LANES)
        b_ref[0, :, sl] = _gated_head_norm(
            o_tok[:, sl], zb_ref[0, :, sl].astype(F32), gob_ref[:, sl]).astype(b_ref.dtype)


SAFE_SPREAD = 100.0


def _attn_kernel(qt_ref, kt_ref, vt_ref, bias_ref, zb_ref, gobcol_ref,
                 sq_ref, skn_ref, svn_ref, ckt_ref, cvt_ref, bc_ref, bn_ref, szb_ref, gob_ref,
                 b_ref, sb_ref, k16, v1, qm, knorm, m_sc, acc_sc):
    i = pl.program_id(1)

    @pl.when(i == 0)
    def _():
        ones = jnp.ones((V_ROWS - HEAD_DIM, TK), BF16)
        ksq_max = jnp.zeros((N_HEADS_B, 1, 1), F32)
        for t in range(k16.shape[0]):
            cols = slice(t * TK, (t + 1) * TK)
            kt = kt_ref[0, :, cols]
            k16[t] = kt.T.astype(BF16)
            ksq = jnp.sum((kt * kt).reshape(N_HEADS_B, HEAD_DIM, TK), axis=1, keepdims=True)
            ksq_max = jnp.maximum(ksq_max, jnp.max(ksq, axis=2, keepdims=True))
            vt = vt_ref[0, :, cols].astype(BF16)
            for h in range(N_HEADS_B):
                v1[t, h, :HEAD_DIM, :] = vt[h * HEAD_DIM:(h + 1) * HEAD_DIM]
                v1[t, h, HEAD_DIM:, :] = ones
        knorm[...] = jnp.broadcast_to(jnp.sqrt(ksq_max), knorm.shape)

    sample_p = _sample_probs(sq_ref, skn_ref, ckt_ref, bc_ref, bn_ref)

    qf = qt_ref[0].astype(F32).reshape(N_HEADS_B, HEAD_DIM, TQ)
    qnorm = jnp.sqrt(jnp.sum(qf * qf, axis=1, keepdims=True))
    spread = qnorm * knorm[...]
    safe = jnp.max(spread) * 2.0 + 4.0 < SAFE_SPREAD
    ref = spread + float(np.log2(len(PATTERNS)))
    per_block = TQ // TK

    row_lo = lax.broadcasted_iota(jnp.int32, (LANES, TQ), 0) < HEAD_DIM
    row_0 = lax.broadcasted_iota(jnp.int32, (LANES, TQ), 0) == 0
    for p in range(D_B // LANES):
        q2 = qt_ref[0, p * LANES:(p + 1) * LANES, :]
        qm[2 * p, :LANES] = jnp.where(row_lo, q2, jnp.zeros_like(q2))
        qm[2 * p + 1, :LANES] = jnp.where(row_lo, jnp.zeros_like(q2), q2)
    for h in range(N_HEADS_B):
        qm[h, LANES:] = jnp.where(row_0, -ref[h], 0.0).astype(BF16)
    key_ones = jnp.ones((TK, LANES), BF16)

    def scores(t, h):
        p = h // 2
        bias = bias_ref[jnp.minimum(per_block * (i + 1) - 1 - t, bias_ref.shape[0] - 1)]
        keys = jnp.concatenate([k16[t, :, p * LANES:(p + 1) * LANES], key_ones], axis=1)
        return jnp.dot(keys, qm[h], preferred_element_type=F32) + bias

    def accumulate_fixed(tiles, first):
        units = [(t, h) for h in range(N_HEADS_B) for t in tiles]
        sts = [scores(t, h) for t, h in units]
        pv = [0.0] * N_HEADS_B
        for u, (t, h) in enumerate(units):
            st = sts[u]
            if u + 1 < len(units):
                head = jnp.maximum(st[:8], jnp.minimum(sts[u + 1][:8], NEG))
                st = jnp.concatenate([head, st[8:]], axis=0)
            pv[h] = pv[h] + jnp.dot(v1[t, h], jnp.exp2(st).astype(BF16),
                                    preferred_element_type=F32)
        for h in range(N_HEADS_B):
            acc_sc[h] = pv[h] if first else acc_sc[h] + pv[h]

    accumulate_fixed([per_block * i + n for n in range(per_block)], first=True)

    @pl.when(safe)
    def _():
        def trip(j, carry):
            accumulate_fixed([2 * per_block * j + n for n in range(2 * per_block)], first=False)
            return carry

        lax.fori_loop(0, i // 2, trip, 0)

        @pl.when(i % 2 == 1)
        def _():
            accumulate_fixed([per_block * (i - 1) + n for n in range(per_block)], first=False)

    @pl.when(jnp.logical_not(safe))
    def _():
        m_sc[...] = jnp.full(m_sc.shape, NEG, F32)
        acc_sc[...] = jnp.zeros(acc_sc.shape, F32)

        def body(t, carry):
            sts = [scores(t, h) for h in range(N_HEADS_B)]
            tops = [jnp.max(st, axis=0, keepdims=True) for st in sts]
            for h in range(N_HEADS_B):
                floor = functools.reduce(jnp.minimum, tops[h:h + 1 + LOOKAHEAD],
                                         jnp.full_like(tops[0], NEG))
                m_old = m_sc[h]
                m_new = jnp.maximum(jnp.maximum(m_old, tops[h]), floor)
                alpha = jnp.exp2(m_old - m_new)
                pt = jnp.exp2(sts[h] - m_new).astype(BF16)
                acc_sc[h] = alpha * acc_sc[h] + jnp.dot(v1[t, h], pt,
                                                        preferred_element_type=F32)
                m_sc[h] = m_new
            return carry

        lax.fori_loop(0, per_block * (i + 1), body, 0)

    _sample_values(*sample_p, svn_ref, cvt_ref, szb_ref, gob_ref, sb_ref)
    acc = acc_sc[...]
    o = acc[:, :HEAD_DIM] / acc[:, HEAD_DIM:HEAD_DIM + 1]
    ms = jnp.sum(o * o, axis=1, keepdims=True) * (1.0 / HEAD_DIM)
    on = o * lax.rsqrt(ms + EPS) * gobcol_ref[...].reshape(N_HEADS_B, HEAD_DIM, 1)
    b_ref[0] = (on.reshape(D_B, TQ).T * _silu(zb_ref[0].astype(F32))).astype(b_ref.dtype)


def _attn_call(qt, kt, vt, zb, sq, sk_new, sv_new, cache_kt, cache_vt, szb, gob):
    bsz, _, s = qt.shape
    dbs, t, _ = sq.shape
    wb = cache_kt.shape[2]
    nt = s // TQ
    assert dbs == bsz * nt, "one sample sequence per prompt query tile"
    near = max(w for w, _ in PATTERNS[:-1])
    diag = TQ // TK - 1
    nkd = (near + TK - 1) // TK + 2 + diag
    qk = np.arange(TQ)[None, :] - np.arange(TK)[:, None]
    bias = jnp.asarray(np.stack([_log2_bias((kd - diag) * TK + qk) for kd in range(nkd)]))
    tq = np.tile(np.arange(t), N_HEADS_B)[:, None]
    bias_c = jnp.asarray(_log2_bias(wb + tq - np.arange(wb)[None, :]))
    d_new = np.where(np.arange(NEW_PAD)[None, :] < t, tq - np.arange(NEW_PAD)[None, :], -1)
    bias_n = jnp.asarray(_log2_bias(d_new))

    qtile = pl.BlockSpec((1, D_B, TQ), lambda b, i: (b, 0, i))
    tile = pl.BlockSpec((1, TQ, D_B), lambda b, i: (b, i, 0))
    whole = pl.BlockSpec((1, D_B, s), lambda b, i: (b, 0, 0))
    tok = pl.BlockSpec((1, t, D_B), lambda b, i: (b * nt + i, 0, 0))
    win = pl.BlockSpec((1, D_B, wb), lambda b, i: (b * nt + i, 0, 0))
    return pl.pallas_call(
        _attn_kernel,
        grid=(bsz, nt),
        in_specs=[qtile, whole, whole, _full(bias.shape), tile, _full((D_B, 1)),
                  tok, tok, tok, win, win, _full(bias_c.shape), _full(bias_n.shape), tok,
                  _full((1, D_B))],
        out_specs=[tile, tok],
        out_shape=[jax.ShapeDtypeStruct((bsz, s, D_B), BF16),
                   jax.ShapeDtypeStruct((dbs, t, D_B), F32)],
        scratch_shapes=[pltpu.VMEM((s // TK, TK, D_B), BF16),
                        pltpu.VMEM((s // TK, N_HEADS_B, V_ROWS, TK), BF16),
                        pltpu.VMEM((N_HEADS_B, 2 * LANES, TQ), BF16),
                        pltpu.VMEM((N_HEADS_B, 1, TQ), F32),
                        pltpu.VMEM((N_HEADS_B, 1, TQ), F32),
                        pltpu.VMEM((N_HEADS_B, V_ROWS, TQ), F32)],
        compiler_params=pltpu.CompilerParams(dimension_semantics=("parallel", "arbitrary"),
                                             vmem_limit_bytes=VMEM_LIMIT_ATTN),
        name="attn",
    )(qt, kt, vt, bias, zb, gob.reshape(D_B, 1),
      sq, sk_new, sv_new, cache_kt, cache_vt, bias_c, bias_n, szb, gob)


def _finish_kernel(x_ref, a_ref, b_ref, p_ref, wout_ref, wple_ref, wgate_ref, gple_ref, y_ref):
    for r0 in range(0, x_ref.shape[0], FIN_SUB):
        rows = slice(r0, r0 + FIN_SUB)
        h = (x_ref[rows, :]
             + jnp.dot(a_ref[rows, :].astype(BF16), wout_ref[:D_A, :], preferred_element_type=F32)
             + jnp.dot(b_ref[rows, :].astype(BF16), wout_ref[D_A:, :], preferred_element_type=F32))
        e = jnp.dot(p_ref[rows, :].astype(BF16), wple_ref[...], preferred_element_type=F32)
        e = e * _rsqrt_mean(e, D_MODEL) * gple_ref[...]
        h16 = h.astype(BF16)
        for c in range(0, D_MODEL, FIN_CHUNK):
            sl = slice(c, c + FIN_CHUNK)
            gate = jax.nn.sigmoid(jnp.dot(h16, wgate_ref[:, sl], preferred_element_type=F32))
            y_ref[rows, sl] = h[:, sl] + gate * e[:, sl]


def _finish_call(x, a, b, p, wout16, wple16, wgate16, gple):
    n = x.shape[0]
    tm = min(TM_FIN, n)
    row = lambda w: pl.BlockSpec((tm, w), lambda i: (i, 0))
    return pl.pallas_call(
        _finish_kernel,
        grid=(n // tm,),
        in_specs=[row(D_MODEL), row(D_A), row(D_B), row(D_PLE), _full(wout16.shape),
                  _full(wple16.shape), _full(wgate16.shape), _full((1, D_MODEL))],
        out_specs=row(D_MODEL),
        out_shape=jax.ShapeDtypeStruct((n, D_MODEL), F32),
        compiler_params=pltpu.CompilerParams(dimension_semantics=("parallel",),
                                             vmem_limit_bytes=VMEM_LIMIT),
        name="finish",
    )(x, a, b, p, wout16, wple16, wgate16, gple)


def _feature_major(c):
    b, p, h, dh = c.shape
    return jnp.transpose(c, (0, 2, 3, 1)).reshape(b, h * dh, p)


def _position_major(ct, h):
    b, hd, p = ct.shape
    return jnp.transpose(ct.reshape(b, h, hd // h, p), (0, 3, 1, 2))


def kernel(x_prompt, x_sample, cache_k, cache_v, p_prompt, p_sample, g_norm, w_in, w_s, b_s,
           g_va, g_oa, g_q, g_k, g_ob, w_out, w_ple, g_ple, w_ple_gate):
    depth = w_in.shape[0]
    assert depth == 1, "single-layer step"
    bsz, seq, _ = x_prompt.shape
    dbs, dseq, _ = x_sample.shape
    assert seq % TQ == 0 and seq <= max(w for w, _ in PATTERNS) and CHUNK % dseq == 0
    i = 0

    gn = g_norm[i].reshape(1, D_MODEL)
    win16 = w_in[i].astype(BF16)
    gva = g_va[i].reshape(1, D_A)
    goa = g_oa[i].reshape(1, D_A)
    gq = jnp.tile(g_q[i], N_HEADS_B).reshape(1, D_B)
    gk = jnp.tile(g_k[i], N_HEADS_B).reshape(1, D_B)
    gob = g_ob[i].reshape(1, D_B)
    wout16 = w_out[i].astype(BF16)
    wple16 = w_ple[i].astype(BF16)
    wgate16 = w_ple_gate[i].astype(BF16)
    gple = g_ple[i].reshape(1, D_MODEL)

    a_p, qt_p, kt_p, vt_p, zb_p = _proj_call(
        x_prompt, gn, win16, w_in[i], w_s[i], b_s[i][:, :, None], gva, goa, gq, gk,
        mix_block=CHUNK, q_dtype=BF16, feature_major=True, emit_vn=False)
    rep = CHUNK // dseq
    wm_s = jnp.tile(w_s[i][:, :dseq, :dseq], (1, rep, rep))
    bm_s = jnp.tile(b_s[i][:, :dseq], (1, rep))[:, :, None]
    xs = x_sample.reshape(1, dbs * dseq, D_MODEL)
    a_s, q_s, k_s, v_s, zb_s, vn_s = _proj_call(
        xs, gn, win16, w_in[i], wm_s, bm_s, gva, goa, gq, gk,
        mix_block=dseq, q_dtype=F32, feature_major=False, emit_vn=True)

    tok = lambda z: z.reshape(dbs, dseq, D_B)
    b_p, b_s_ = _attn_call(qt_p, kt_p, vt_p, zb_p, tok(q_s), tok(k_s), tok(v_s),
                           _feature_major(cache_k[i]), _feature_major(cache_v[i]), tok(zb_s), gob)

    y_p = _finish_call(x_prompt.reshape(bsz * seq, D_MODEL), a_p.reshape(bsz * seq, D_A),
                       b_p.reshape(bsz * seq, D_B), p_prompt[i].reshape(bsz * seq, D_PLE),
                       wout16, wple16, wgate16, gple)
    y_s = _finish_call(xs.reshape(dbs * dseq, D_MODEL), a_s.reshape(dbs * dseq, D_A),
                       b_s_.reshape(dbs * dseq, D_B), p_sample[i].reshape(dbs * dseq, D_PLE),
                       wout16, wple16, wgate16, gple)

    hshape = (N_HEADS_B, HEAD_DIM)
    return (y_p.reshape(bsz, seq, D_MODEL),
            y_s.reshape(dbs, dseq, D_MODEL),
            _position_major(kt_p, N_HEADS_B)[None],
            _position_major(vt_p, N_HEADS_B)[None],
            k_s.reshape(1, dbs, dseq, *hshape),
            v_s.reshape(1, dbs, dseq, *hshape),
            vn_s.reshape(1, dbs, dseq, D_A))
```

```python
import functools

import numpy as np
import jax
import jax.numpy as jnp
from jax import lax
from jax.experimental import pallas as pl
from jax.experimental.pallas import tpu as pltpu

D_MODEL = 1024
D_A = 512
N_GROUPS_A = 4
GROUP_A = 128
CHUNK = 128
D_B = 512
N_HEADS_B = 8
HEAD_DIM = 64
PATTERNS = ((128, 1), (512, 4), (2048, 16))
D_PLE = 256
EPS = 1e-6
NEG = -1e30
SCALE = HEAD_DIM ** -0.5
Q_SCALE = SCALE * float(np.log2(np.e))
SEG = 512
SEG_U, SEG_VA, SEG_ZA, SEG_Q, SEG_K, SEG_VB, SEG_ZB = range(7)

LANES = 128
VMEM_LIMIT = 48 * 1024 * 1024
VMEM_LIMIT_ATTN = 56 * 1024 * 1024

TM_PROJ = 1024
PROJ_SUB = 256
TM_FIN = 1024
FIN_SUB = 256
FIN_CHUNK = 256
TQ = 256
TK = 256

BF16 = jnp.bfloat16
F32 = jnp.float32
NT = (((1,), (1,)), ((), ()))


def _multiplicity(d):
    c = np.zeros(d.shape, np.int32)
    for w, r in PATTERNS:
        c += ((d >= 0) & (d <= w) & (d % r == 0)).astype(np.int32)
    return c


def _log2_bias(d):
    c = _multiplicity(d)
    return np.where(c > 0, np.log2(np.maximum(c, 1).astype(np.float64)), NEG).astype(np.float32)


def _rsqrt_mean(x, n):
    return lax.rsqrt(jnp.sum(x * x, axis=-1, keepdims=True) * (1.0 / n) + EPS)


def _head_rms_scale(x):
    lane_lo = lax.broadcasted_iota(jnp.int32, x.shape, 1) < HEAD_DIM
    x2 = x * x
    lo = jnp.sum(jnp.where(lane_lo, x2, 0.0), axis=-1, keepdims=True)
    hi = jnp.sum(jnp.where(lane_lo, 0.0, x2), axis=-1, keepdims=True)
    ms = jnp.where(lane_lo, lo, hi) * (1.0 / HEAD_DIM)
    return lax.rsqrt(ms + EPS)


def _silu(z):
    return z * jax.nn.sigmoid(z)


def _full(shape):
    return pl.BlockSpec(shape, lambda *_: (0,) * len(shape))


def _head_norm_feature_major(xt, gcol_ref):
    m = xt.shape[1]
    x3 = xt.reshape(N_HEADS_B, HEAD_DIM, m)
    ms = jnp.sum(x3 * x3, axis=1, keepdims=True) * (1.0 / HEAD_DIM)
    g3 = gcol_ref[...].reshape(N_HEADS_B, HEAD_DIM, 1)
    return (x3 * lax.rsqrt(ms + EPS) * g3).reshape(D_B, m)


def _proj_kernel(*refs, mix_block, feature_major, emit_vn):
    refs = list(refs)
    x_ref, gn_ref, win_ref = refs[:3]
    del refs[:3]
    wqkv_ref = refs.pop(0) if feature_major else None
    (wm_ref, bm_ref, gva_ref, goa_ref, gq_ref, gk_ref, gqcol_ref, gkcol_ref,
     a_ref, q_ref, k_ref, v_ref, zb_ref) = refs[:13]
    del refs[:13]
    vn_ref = refs.pop(0) if emit_vn else None
    wqkvt = refs.pop(0) if feature_major else None

    if feature_major:
        @pl.when((pl.program_id(0) == 0) & (pl.program_id(1) == 0))
        def _():
            for c in range(0, wqkvt.shape[0], 2 * LANES):
                wqkvt[c:c + 2 * LANES, :] = wqkv_ref[:, c:c + 2 * LANES].T.astype(BF16)

    row = lax.broadcasted_iota(jnp.int32, (CHUNK, CHUNK), 0)
    col = lax.broadcasted_iota(jnp.int32, (CHUNK, CHUNK), 1)
    mix_mask = (row >= col) & ((row // mix_block) == (col // mix_block))
    wms = [jnp.where(mix_mask, wm_ref[g], 0.0).astype(BF16) for g in range(N_GROUPS_A)]

    for r0 in range(0, x_ref.shape[1], PROJ_SUB):
        rows = slice(r0, r0 + PROJ_SUB)
        x = x_ref[0, rows, :]
        h = (x * _rsqrt_mean(x, D_MODEL) * gn_ref[...]).astype(BF16)

        def seg(i):
            return jnp.dot(h, win_ref[:, i * SEG:(i + 1) * SEG], preferred_element_type=F32)

        zb_ref[0, rows, :] = seg(SEG_ZB).astype(zb_ref.dtype)
        if feature_major:
            qkvt = lax.dot_general(wqkvt[...], h, NT, preferred_element_type=F32)
            q_ref[0, :, rows] = (_head_norm_feature_major(qkvt[:D_B], gqcol_ref)
                                 * Q_SCALE).astype(q_ref.dtype)
            k_ref[0, :, rows] = _head_norm_feature_major(qkvt[D_B:2 * D_B], gkcol_ref)
            v_ref[0, :, rows] = qkvt[2 * D_B:]
        else:
            q, k = seg(SEG_Q), seg(SEG_K)
            v_ref[0, rows, :] = seg(SEG_VB)
            for j in range(D_B // LANES):
                sl = slice(j * LANES, (j + 1) * LANES)
                qs, ks = q[:, sl], k[:, sl]
                q_ref[0, rows, sl] = (qs * _head_rms_scale(qs) * gq_ref[:, sl]
                                      * Q_SCALE).astype(q_ref.dtype)
                k_ref[0, rows, sl] = ks * _head_rms_scale(ks) * gk_ref[:, sl]

        u, va, za = seg(SEG_U), seg(SEG_VA), seg(SEG_ZA)
        for g in range(N_GROUPS_A):
            sl = slice(g * GROUP_A, (g + 1) * GROUP_A)
            vs = va[:, sl]
            vn = vs * _rsqrt_mean(vs, GROUP_A) * gva_ref[:, sl]
            if emit_vn:
                vn_ref[0, rows, sl] = vn
            vn16 = vn.astype(BF16)
            mixed = jnp.concatenate(
                [jnp.dot(wms[g], vn16[c * CHUNK:(c + 1) * CHUNK], preferred_element_type=F32)
                 + bm_ref[g] for c in range(PROJ_SUB // CHUNK)], axis=0)
            a = u[:, sl] * mixed
            a = a * _rsqrt_mean(a, GROUP_A) * goa_ref[:, sl]
            a_ref[0, rows, sl] = (a * _silu(za[:, sl])).astype(a_ref.dtype)


def _proj_call(x, gn, win16, w_in, wm, bm, gva, goa, gq, gk, *, mix_block, q_dtype,
               feature_major, emit_vn):
    g, r, _ = x.shape
    tm = min(TM_PROJ, r)
    row = lambda w: pl.BlockSpec((1, tm, w), lambda b, i: (b, i, 0))
    n_qkv = SEG_VB + 1 - SEG_Q
    assert SEG_Q % n_qkv == 0, "q | k | v must be one aligned column block of w_in"
    if feature_major:
        qkv_shape, qkv_spec = (g, D_B, r), pl.BlockSpec((1, D_B, tm), lambda b, i: (b, 0, i))
        w_args = [win16, w_in]
        w_specs = [_full(win16.shape),
                   pl.BlockSpec((D_MODEL, n_qkv * SEG), lambda b, i: (0, SEG_Q // n_qkv))]
        scratch = [pltpu.VMEM((n_qkv * SEG, D_MODEL), BF16)]
    else:
        qkv_shape, qkv_spec = (g, r, D_B), row(D_B)
        w_args, w_specs, scratch = [win16], [_full(win16.shape)], []
    out_shape = [jax.ShapeDtypeStruct((g, r, D_A), BF16),
                 jax.ShapeDtypeStruct(qkv_shape, q_dtype),
                 jax.ShapeDtypeStruct(qkv_shape, F32),
                 jax.ShapeDtypeStruct(qkv_shape, F32),
                 jax.ShapeDtypeStruct((g, r, D_B), BF16)]
    out_specs = [row(D_A), qkv_spec, qkv_spec, qkv_spec, row(D_B)]
    if emit_vn:
        out_shape.append(jax.ShapeDtypeStruct((g, r, D_A), F32))
        out_specs.append(row(D_A))
    return pl.pallas_call(
        functools.partial(_proj_kernel, mix_block=mix_block, feature_major=feature_major,
                          emit_vn=emit_vn),
        grid=(g, r // tm),
        in_specs=[row(D_MODEL), _full((1, D_MODEL)), *w_specs,
                  _full(wm.shape), _full(bm.shape), _full((1, D_A)), _full((1, D_A)),
                  _full((1, D_B)), _full((1, D_B)), _full((D_B, 1)), _full((D_B, 1))],
        out_specs=out_specs,
        out_shape=out_shape,
        scratch_shapes=scratch,
        compiler_params=pltpu.CompilerParams(dimension_semantics=("arbitrary", "arbitrary"),
                                             vmem_limit_bytes=VMEM_LIMIT),
        name="proj",
    )(x, gn, *w_args, wm, bm, gva, goa, gq, gk, gq.reshape(D_B, 1), gk.reshape(D_B, 1))


def _gated_head_norm(o, zb, gob):
    return o * _head_rms_scale(o) * gob * _silu(zb)


V_ROWS = HEAD_DIM + 16
LOOKAHEAD = 1
NEW_PAD = LANES


def _sample_head_mask(t):
    r_i = lax.broadcasted_iota(jnp.int32, (N_HEADS_B * t, D_B), 0)
    c_i = lax.broadcasted_iota(jnp.int32, (N_HEADS_B * t, D_B), 1)
    return (r_i // t) == (c_i // HEAD_DIM)


def _sample_probs(q_ref, kn_ref, ckt_ref, bc_ref, bn_ref):
    t = q_ref.shape[1]
    q_rep = jnp.concatenate([q_ref[0]] * N_HEADS_B, axis=0)
    q_bd = jnp.where(_sample_head_mask(t), q_rep, 0.0).astype(BF16)
    pad = jnp.zeros((NEW_PAD - t, D_B), F32)
    k_new = jnp.concatenate([kn_ref[0], pad], axis=0).astype(BF16)

    s_c = jnp.dot(q_bd.astype(F32), ckt_ref[0], preferred_element_type=F32) + bc_ref[...]
    s_n = lax.dot_general(q_bd, k_new, NT, preferred_element_type=F32) + bn_ref[...]
    m = jnp.maximum(jnp.max(s_c, axis=-1, keepdims=True), jnp.max(s_n, axis=-1, keepdims=True))
    p_c = jnp.exp2(s_c - m)
    p_n = jnp.exp2(s_n - m)
    return p_c, p_n


def _sample_values(p_c, p_n, vn_ref, cvt_ref, zb_ref, gob_ref, b_ref):
    t = vn_ref.shape[1]
    pad = jnp.zeros((NEW_PAD - t, D_B), F32)
    v_new = jnp.concatenate([vn_ref[0], pad], axis=0).astype(BF16)
    l = jnp.sum(p_c, axis=-1, keepdims=True) + jnp.sum(p_n, axis=-1, keepdims=True)
    o = (lax.dot_general(p_c, cvt_ref[0], NT, preferred_element_type=F32)
         + jnp.dot(p_n.astype(BF16), v_new, preferred_element_type=F32)) / l
    o = jnp.where(_sample_head_mask(t), o, 0.0)
    o_tok = o[0:t]
    for hd in range(1, N_HEADS_B):
        o_tok = o_tok + o[hd * t:(hd + 1) * t]
    for p in range(D_B // LANES):
        sl = slice(p * LANES, (p + 1) * LANES)
        b_ref[0, :, sl] = _gated_head_norm(
            o_tok[:, sl], zb_ref[0, :, sl].astype(F32), gob_ref[:, sl]).astype(b_ref.dtype)


SAFE_SPREAD = 100.0


def _attn_kernel(qt_ref, kt_ref, vt_ref, bias_ref, zb_ref, gobcol_ref,
                 sq_ref, skn_ref, svn_ref, ckt_ref, cvt_ref, bc_ref, bn_ref, szb_ref, gob_ref,
                 b_ref, sb_ref, k16, v1, qm, knorm, m_sc, acc_sc):
    i = pl.program_id(1)

    @pl.when(i == 0)
    def _():
        ones = jnp.ones((V_ROWS - HEAD_DIM, TK), BF16)
        ksq_max = jnp.zeros((N_HEADS_B, 1, 1), F32)
        for t in range(k16.shape[0]):
            cols = slice(t * TK, (t + 1) * TK)
            kt = kt_ref[0, :, cols]
            k16[t] = kt.T.astype(BF16)
            ksq = jnp.sum((kt * kt).reshape(N_HEADS_B, HEAD_DIM, TK), axis=1, keepdims=True)
            ksq_max = jnp.maximum(ksq_max, jnp.max(ksq, axis=2, keepdims=True))
            vt = vt_ref[0, :, cols].astype(BF16)
            for h in range(N_HEADS_B):
                v1[t, h, :HEAD_DIM, :] = vt[h * HEAD_DIM:(h + 1) * HEAD_DIM]
                v1[t, h, HEAD_DIM:, :] = ones
        knorm[...] = jnp.broadcast_to(jnp.sqrt(ksq_max), knorm.shape)

    sample_p = _sample_probs(sq_ref, skn_ref, ckt_ref, bc_ref, bn_ref)

    qf = qt_ref[0].astype(F32).reshape(N_HEADS_B, HEAD_DIM, TQ)
    qnorm = jnp.sqrt(jnp.sum(qf * qf, axis=1, keepdims=True))
    spread = qnorm * knorm[...]
    safe = jnp.max(spread) * 2.0 + 4.0 < SAFE_SPREAD
    ref = spread + float(np.log2(len(PATTERNS)))
    per_block = TQ // TK

    row_lo = lax.broadcasted_iota(jnp.int32, (LANES, TQ), 0) < HEAD_DIM
    row_0 = lax.broadcasted_iota(jnp.int32, (LANES, TQ), 0) == 0
    for p in range(D_B // LANES):
        q2 = qt_ref[0, p * LANES:(p + 1) * LANES, :]
        qm[2 * p, :LANES] = jnp.where(row_lo, q2, jnp.zeros_like(q2))
        qm[2 * p + 1, :LANES] = jnp.where(row_lo, jnp.zeros_like(q2), q2)
    for h in range(N_HEADS_B):
        qm[h, LANES:] = jnp.where(row_0, -ref[h], 0.0).astype(BF16)
    key_ones = jnp.ones((TK, LANES), BF16)

    def scores(t, h):
        p = h // 2
        bias = bias_ref[jnp.minimum(per_block * (i + 1) - 1 - t, bias_ref.shape[0] - 1)]
        keys = jnp.concatenate([k16[t, :, p * LANES:(p + 1) * LANES], key_ones], axis=1)
        return jnp.dot(keys, qm[h], preferred_element_type=F32) + bias

    def accumulate_fixed(tiles, first):
        units = [(t, h) for h in range(N_HEADS_B) for t in tiles]
        sts = []
        for h in range(N_HEADS_B):
            p = h // 2
            keys = jnp.concatenate(
                [jnp.concatenate([k16[t, :, p * LANES:(p + 1) * LANES], key_ones], axis=1)
                 for t in tiles], axis=0)
            both = jnp.dot(keys, qm[h], preferred_element_type=F32)
            for n, t in enumerate(tiles):
                bias = bias_ref[jnp.minimum(per_block * (i + 1) - 1 - t, bias_ref.shape[0] - 1)]
                sts.append(both[n * TK:(n + 1) * TK] + bias)
        pv = [0.0] * N_HEADS_B
        for u, (t, h) in enumerate(units):
            st = sts[u]
            if u + 1 < len(units):
                head = jnp.maximum(st[:8], jnp.minimum(sts[u + 1][:8], NEG))
                st = jnp.concatenate([head, st[8:]], axis=0)
            pv[h] = pv[h] + jnp.dot(v1[t, h], jnp.exp2(st).astype(BF16),
                                    preferred_element_type=F32)
        for h in range(N_HEADS_B):
            acc_sc[h] = pv[h] if first else acc_sc[h] + pv[h]

    accumulate_fixed([per_block * i + n for n in range(per_block)], first=True)

    @pl.when(safe)
    def _():
        def trip(j, carry):
            accumulate_fixed([2 * per_block * j + n for n in range(2 * per_block)], first=False)
            return carry

        lax.fori_loop(0, i // 2, trip, 0)

        @pl.when(i % 2 == 1)
        def _():
            accumulate_fixed([per_block * (i - 1) + n for n in range(per_block)], first=False)

    @pl.when(jnp.logical_not(safe))
    def _():
        m_sc[...] = jnp.full(m_sc.shape, NEG, F32)
        acc_sc[...] = jnp.zeros(acc_sc.shape, F32)

        def body(t, carry):
            sts = [scores(t, h) for h in range(N_HEADS_B)]
            tops = [jnp.max(st, axis=0, keepdims=True) for st in sts]
            for h in range(N_HEADS_B):
                floor = functools.reduce(jnp.minimum, tops[h:h + 1 + LOOKAHEAD],
                                         jnp.full_like(tops[0], NEG))
                m_old = m_sc[h]
                m_new = jnp.maximum(jnp.maximum(m_old, tops[h]), floor)
                alpha = jnp.exp2(m_old - m_new)
                pt = jnp.exp2(sts[h] - m_new).astype(BF16)
                acc_sc[h] = alpha * acc_sc[h] + jnp.dot(v1[t, h], pt,
                                                        preferred_element_type=F32)
                m_sc[h] = m_new
            return carry

        lax.fori_loop(0, per_block * (i + 1), body, 0)

    _sample_values(*sample_p, svn_ref, cvt_ref, szb_ref, gob_ref, sb_ref)
    acc = acc_sc[...]
    o = acc[:, :HEAD_DIM] / acc[:, HEAD_DIM:HEAD_DIM + 1]
    ms = jnp.sum(o * o, axis=1, keepdims=True) * (1.0 / HEAD_DIM)
    on = o * lax.rsqrt(ms + EPS) * gobcol_ref[...].reshape(N_HEADS_B, HEAD_DIM, 1)
    b_ref[0] = (on.reshape(D_B, TQ).T * _silu(zb_ref[0].astype(F32))).astype(b_ref.dtype)


def _attn_call(qt, kt, vt, zb, sq, sk_new, sv_new, cache_kt, cache_vt, szb, gob):
    bsz, _, s = qt.shape
    dbs, t, _ = sq.shape
    wb = cache_kt.shape[2]
    nt = s // TQ
    assert dbs == bsz * nt, "one sample sequence per prompt query tile"
    near = max(w for w, _ in PATTERNS[:-1])
    diag = TQ // TK - 1
    nkd = (near + TK - 1) // TK + 2 + diag
    qk = np.arange(TQ)[None, :] - np.arange(TK)[:, None]
    bias = jnp.asarray(np.stack([_log2_bias((kd - diag) * TK + qk) for kd in range(nkd)]))
    tq = np.tile(np.arange(t), N_HEADS_B)[:, None]
    bias_c = jnp.asarray(_log2_bias(wb + tq - np.arange(wb)[None, :]))
    d_new = np.where(np.arange(NEW_PAD)[None, :] < t, tq - np.arange(NEW_PAD)[None, :], -1)
    bias_n = jnp.asarray(_log2_bias(d_new))

    qtile = pl.BlockSpec((1, D_B, TQ), lambda b, i: (b, 0, i))
    tile = pl.BlockSpec((1, TQ, D_B), lambda b, i: (b, i, 0))
    whole = pl.BlockSpec((1, D_B, s), lambda b, i: (b, 0, 0))
    tok = pl.BlockSpec((1, t, D_B), lambda b, i: (b * nt + i, 0, 0))
    win = pl.BlockSpec((1, D_B, wb), lambda b, i: (b * nt + i, 0, 0))
    return pl.pallas_call(
        _attn_kernel,
        grid=(bsz, nt),
        in_specs=[qtile, whole, whole, _full(bias.shape), tile, _full((D_B, 1)),
                  tok, tok, tok, win, win, _full(bias_c.shape), _full(bias_n.shape), tok,
                  _full((1, D_B))],
        out_specs=[tile, tok],
        out_shape=[jax.ShapeDtypeStruct((bsz, s, D_B), BF16),
                   jax.ShapeDtypeStruct((dbs, t, D_B), F32)],
        scratch_shapes=[pltpu.VMEM((s // TK, TK, D_B), BF16),
                        pltpu.VMEM((s // TK, N_HEADS_B, V_ROWS, TK), BF16),
                        pltpu.VMEM((N_HEADS_B, 2 * LANES, TQ), BF16),
                        pltpu.VMEM((N_HEADS_B, 1, TQ), F32),
                        pltpu.VMEM((N_HEADS_B, 1, TQ), F32),
                        pltpu.VMEM((N_HEADS_B, V_ROWS, TQ), F32)],
        compiler_params=pltpu.CompilerParams(dimension_semantics=("parallel", "arbitrary"),
                                             vmem_limit_bytes=VMEM_LIMIT_ATTN),
        name="attn",
    )(qt, kt, vt, bias, zb, gob.reshape(D_B, 1),
      sq, sk_new, sv_new, cache_kt, cache_vt, bias_c, bias_n, szb, gob)


def _finish_kernel(x_ref, a_ref, b_ref, p_ref, wout_ref, wple_ref, wgate_ref, gple_ref, y_ref):
    for r0 in range(0, x_ref.shape[0], FIN_SUB):
        rows = slice(r0, r0 + FIN_SUB)
        h = (x_ref[rows, :]
             + jnp.dot(a_ref[rows, :].astype(BF16), wout_ref[:D_A, :], preferred_element_type=F32)
             + jnp.dot(b_ref[rows, :].astype(BF16), wout_ref[D_A:, :], preferred_element_type=F32))
        e = jnp.dot(p_ref[rows, :].astype(BF16), wple_ref[...], preferred_element_type=F32)
        e = e * _rsqrt_mean(e, D_MODEL) * gple_ref[...]
        h16 = h.astype(BF16)
        for c in range(0, D_MODEL, FIN_CHUNK):
            sl = slice(c, c + FIN_CHUNK)
            gate = jax.nn.sigmoid(jnp.dot(h16, wgate_ref[:, sl], preferred_element_type=F32))
            y_ref[rows, sl] = h[:, sl] + gate * e[:, sl]


def _finish_call(x, a, b, p, wout16, wple16, wgate16, gple):
    n = x.shape[0]
    tm = min(TM_FIN, n)
    row = lambda w: pl.BlockSpec((tm, w), lambda i: (i, 0))
    return pl.pallas_call(
        _finish_kernel,
        grid=(n // tm,),
        in_specs=[row(D_MODEL), row(D_A), row(D_B), row(D_PLE), _full(wout16.shape),
                  _full(wple16.shape), _full(wgate16.shape), _full((1, D_MODEL))],
        out_specs=row(D_MODEL),
        out_shape=jax.ShapeDtypeStruct((n, D_MODEL), F32),
        compiler_params=pltpu.CompilerParams(dimension_semantics=("parallel",),
                                             vmem_limit_bytes=VMEM_LIMIT),
        name="finish",
    )(x, a, b, p, wout16, wple16, wgate16, gple)


def _feature_major(c):
    b, p, h, dh = c.shape
    return jnp.transpose(c, (0, 2, 3, 1)).reshape(b, h * dh, p)


def _position_major(ct, h):
    b, hd, p = ct.shape
    return jnp.transpose(ct.reshape(b, h, hd // h, p), (0, 3, 1, 2))


def kernel(x_prompt, x_sample, cache_k, cache_v, p_prompt, p_sample, g_norm, w_in, w_s, b_s,
           g_va, g_oa, g_q, g_k, g_ob, w_out, w_ple, g_ple, w_ple_gate):
    depth = w_in.shape[0]
    assert depth == 1, "single-layer step"
    bsz, seq, _ = x_prompt.shape
    dbs, dseq, _ = x_sample.shape
    assert seq % TQ == 0 and seq <= max(w for w, _ in PATTERNS) and CHUNK % dseq == 0
    i = 0

    gn = g_norm[i].reshape(1, D_MODEL)
    win16 = w_in[i].astype(BF16)
    gva = g_va[i].reshape(1, D_A)
    goa = g_oa[i].reshape(1, D_A)
    gq = jnp.tile(g_q[i], N_HEADS_B).reshape(1, D_B)
    gk = jnp.tile(g_k[i], N_HEADS_B).reshape(1, D_B)
    gob = g_ob[i].reshape(1, D_B)
    wout16 = w_out[i].astype(BF16)
    wple16 = w_ple[i].astype(BF16)
    wgate16 = w_ple_gate[i].astype(BF16)
    gple = g_ple[i].reshape(1, D_MODEL)

    a_p, qt_p, kt_p, vt_p, zb_p = _proj_call(
        x_prompt, gn, win16, w_in[i], w_s[i], b_s[i][:, :, None], gva, goa, gq, gk,
        mix_block=CHUNK, q_dtype=BF16, feature_major=True, emit_vn=False)
    rep = CHUNK // dseq
    wm_s = jnp.tile(w_s[i][:, :dseq, :dseq], (1, rep, rep))
    bm_s = jnp.tile(b_s[i][:, :dseq], (1, rep))[:, :, None]
    xs = x_sample.reshape(1, dbs * dseq, D_MODEL)
    a_s, q_s, k_s, v_s, zb_s, vn_s = _proj_call(
        xs, gn, win16, w_in[i], wm_s, bm_s, gva, goa, gq, gk,
        mix_block=dseq, q_dtype=F32, feature_major=False, emit_vn=True)

    tok = lambda z: z.reshape(dbs, dseq, D_B)
    b_p, b_s_ = _attn_call(qt_p, kt_p, vt_p, zb_p, tok(q_s), tok(k_s), tok(v_s),
                           _feature_major(cache_k[i]), _feature_major(cache_v[i]), tok(zb_s), gob)

    y_p = _finish_call(x_prompt.reshape(bsz * seq, D_MODEL), a_p.reshape(bsz * seq, D_A),
                       b_p.reshape(bsz * seq, D_B), p_prompt[i].reshape(bsz * seq, D_PLE),
                       wout16, wple16, wgate16, gple)
    y_s = _finish_call(xs.reshape(dbs * dseq, D_MODEL), a_s.reshape(dbs * dseq, D_A),
                       b_s_.reshape(dbs * dseq, D_B), p_sample[i].reshape(dbs * dseq, D_PLE),
                       wout16, wple16, wgate16, gple)

    hshape = (N_HEADS_B, HEAD_DIM)
    return (y_p.reshape(bsz, seq, D_MODEL),
            y_s.reshape(dbs, dseq, D_MODEL),
            _position_major(kt_p, N_HEADS_B)[None],
            _position_major(vt_p, N_HEADS_B)[None],
            k_s.reshape(1, dbs, dseq, *hshape),
            v_s.reshape(1, dbs, dseq, *hshape),
            vn_s.reshape(1, dbs, dseq, D_A))
```

```python
import functools

import numpy as np
import jax
import jax.numpy as jnp
from jax import lax
from jax.experimental import pallas as pl
from jax.experimental.pallas import tpu as pltpu

D_MODEL = 1024
D_A = 512
N_GROUPS_A = 4
GROUP_A = 128
CHUNK = 128
D_B = 512
N_HEADS_B = 8
HEAD_DIM = 64
PATTERNS = ((128, 1), (512, 4), (2048, 16))
D_PLE = 256
EPS = 1e-6
NEG = -1e30
SCALE = HEAD_DIM ** -0.5
Q_SCALE = SCALE * float(np.log2(np.e))
SEG = 512
SEG_U, SEG_VA, SEG_ZA, SEG_Q, SEG_K, SEG_VB, SEG_ZB = range(7)

LANES = 128
VMEM_LIMIT = 48 * 1024 * 1024
VMEM_LIMIT_ATTN = 56 * 1024 * 1024

TM_PROJ = 1024
PROJ_SUB = 256
TM_FIN = 1024
FIN_SUB = 256
FIN_CHUNK = 256
TQ = 256
TK = 256

BF16 = jnp.bfloat16
F32 = jnp.float32
NT = (((1,), (1,)), ((), ()))


def _multiplicity(d):
    c = np.zeros(d.shape, np.int32)
    for w, r in PATTERNS:
        c += ((d >= 0) & (d <= w) & (d % r == 0)).astype(np.int32)
    return c


def _log2_bias(d):
    c = _multiplicity(d)
    return np.where(c > 0, np.log2(np.maximum(c, 1).astype(np.float64)), NEG).astype(np.float32)


def _rsqrt_mean(x, n):
    return lax.rsqrt(jnp.sum(x * x, axis=-1, keepdims=True) * (1.0 / n) + EPS)


def _head_rms_scale(x):
    lane_lo = lax.broadcasted_iota(jnp.int32, x.shape, 1) < HEAD_DIM
    x2 = x * x
    lo = jnp.sum(jnp.where(lane_lo, x2, 0.0), axis=-1, keepdims=True)
    hi = jnp.sum(jnp.where(lane_lo, 0.0, x2), axis=-1, keepdims=True)
    ms = jnp.where(lane_lo, lo, hi) * (1.0 / HEAD_DIM)
    return lax.rsqrt(ms + EPS)


def _silu(z):
    return z * jax.nn.sigmoid(z)


def _full(shape):
    return pl.BlockSpec(shape, lambda *_: (0,) * len(shape))


def _head_norm_feature_major(xt, gcol_ref):
    m = xt.shape[1]
    x3 = xt.reshape(N_HEADS_B, HEAD_DIM, m)
    ms = jnp.sum(x3 * x3, axis=1, keepdims=True) * (1.0 / HEAD_DIM)
    g3 = gcol_ref[...].reshape(N_HEADS_B, HEAD_DIM, 1)
    return (x3 * lax.rsqrt(ms + EPS) * g3).reshape(D_B, m)


def _proj_kernel(*refs, mix_block, feature_major, emit_vn, new_len):
    refs = list(refs)
    x_ref, gn_ref, win_ref = refs[:3]
    del refs[:3]
    wqkv_ref = refs.pop(0) if feature_major else None
    (wm_ref, bm_ref, gva_ref, goa_ref, gq_ref, gk_ref, gqcol_ref, gkcol_ref,
     a_ref, q_ref, k_ref, v_ref, zb_ref) = refs[:13]
    del refs[:13]
    vn_ref = refs.pop(0) if emit_vn else None
    knew_ref, vnew_ref = (refs.pop(0), refs.pop(0)) if new_len else (None, None)
    wqkvt = refs.pop(0) if feature_major else None

    if feature_major:
        @pl.when((pl.program_id(0) == 0) & (pl.program_id(1) == 0))
        def _():
            for c in range(0, wqkvt.shape[0], 2 * LANES):
                wqkvt[c:c + 2 * LANES, :] = wqkv_ref[:, c:c + 2 * LANES].T.astype(BF16)

    row = lax.broadcasted_iota(jnp.int32, (CHUNK, CHUNK), 0)
    col = lax.broadcasted_iota(jnp.int32, (CHUNK, CHUNK), 1)
    mix_mask = (row >= col) & ((row // mix_block) == (col // mix_block))
    wms = [jnp.where(mix_mask, wm_ref[g], 0.0).astype(BF16) for g in range(N_GROUPS_A)]

    for r0 in range(0, x_ref.shape[1], PROJ_SUB):
        rows = slice(r0, r0 + PROJ_SUB)
        x = x_ref[0, rows, :]
        h = (x * _rsqrt_mean(x, D_MODEL) * gn_ref[...]).astype(BF16)

        def seg(i):
            return jnp.dot(h, win_ref[:, i * SEG:(i + 1) * SEG], preferred_element_type=F32)

        zb_ref[0, rows, :] = seg(SEG_ZB).astype(zb_ref.dtype)
        if feature_major:
            qkvt = lax.dot_general(wqkvt[...], h, NT, preferred_element_type=F32)
            q_ref[0, :, rows] = (_head_norm_feature_major(qkvt[:D_B], gqcol_ref)
                                 * Q_SCALE).astype(q_ref.dtype)
            k_ref[0, :, rows] = _head_norm_feature_major(qkvt[D_B:2 * D_B], gkcol_ref)
            v_ref[0, :, rows] = qkvt[2 * D_B:]
        else:
            q, k = seg(SEG_Q), seg(SEG_K)
            v_ref[0, rows, :] = seg(SEG_VB)
            for j in range(D_B // LANES):
                sl = slice(j * LANES, (j + 1) * LANES)
                qs, ks = q[:, sl], k[:, sl]
                q_ref[0, rows, sl] = (qs * _head_rms_scale(qs) * gq_ref[:, sl]
                                      * Q_SCALE).astype(q_ref.dtype)
                k_ref[0, rows, sl] = ks * _head_rms_scale(ks) * gk_ref[:, sl]

        u, va, za = seg(SEG_U), seg(SEG_VA), seg(SEG_ZA)
        for g in range(N_GROUPS_A):
            sl = slice(g * GROUP_A, (g + 1) * GROUP_A)
            vs = va[:, sl]
            vn = vs * _rsqrt_mean(vs, GROUP_A) * gva_ref[:, sl]
            if emit_vn:
                vn_ref[0, rows, sl] = vn
            vn16 = vn.astype(BF16)
            mixed = jnp.concatenate(
                [jnp.dot(wms[g], vn16[c * CHUNK:(c + 1) * CHUNK], preferred_element_type=F32)
                 + bm_ref[g] for c in range(PROJ_SUB // CHUNK)], axis=0)
            a = u[:, sl] * mixed
            a = a * _rsqrt_mean(a, GROUP_A) * goa_ref[:, sl]
            a_ref[0, rows, sl] = (a * _silu(za[:, sl])).astype(a_ref.dtype)

    if new_len:
        n_seq = x_ref.shape[1] // new_len
        k3 = k_ref[0].reshape(n_seq, new_len, D_B)
        v3 = v_ref[0].reshape(n_seq, new_len, D_B)
        for t in range(new_len):
            knew_ref[t] = k3[:, t, :].T
            vnew_ref[t] = v3[:, t, :].T


def _proj_call(x, gn, win16, w_in, wm, bm, gva, goa, gq, gk, *, mix_block, q_dtype,
               feature_major, emit_vn, new_len=0):
    g, r, _ = x.shape
    tm = min(TM_PROJ, r)
    row = lambda w: pl.BlockSpec((1, tm, w), lambda b, i: (b, i, 0))
    n_qkv = SEG_VB + 1 - SEG_Q
    assert SEG_Q % n_qkv == 0, "q | k | v must be one aligned column block of w_in"
    if feature_major:
        qkv_shape, qkv_spec = (g, D_B, r), pl.BlockSpec((1, D_B, tm), lambda b, i: (b, 0, i))
        w_args = [win16, w_in]
        w_specs = [_full(win16.shape),
                   pl.BlockSpec((D_MODEL, n_qkv * SEG), lambda b, i: (0, SEG_Q // n_qkv))]
        scratch = [pltpu.VMEM((n_qkv * SEG, D_MODEL), BF16)]
    else:
        qkv_shape, qkv_spec = (g, r, D_B), row(D_B)
        w_args, w_specs, scratch = [win16], [_full(win16.shape)], []
    out_shape = [jax.ShapeDtypeStruct((g, r, D_A), BF16),
                 jax.ShapeDtypeStruct(qkv_shape, q_dtype),
                 jax.ShapeDtypeStruct(qkv_shape, F32),
                 jax.ShapeDtypeStruct(qkv_shape, F32),
                 jax.ShapeDtypeStruct((g, r, D_B), BF16)]
    out_specs = [row(D_A), qkv_spec, qkv_spec, qkv_spec, row(D_B)]
    if emit_vn:
        out_shape.append(jax.ShapeDtypeStruct((g, r, D_A), F32))
        out_specs.append(row(D_A))
    if new_len:
        assert g == 1 and tm == r, "new-row layout needs all rows in one grid step"
        out_shape += [jax.ShapeDtypeStruct((new_len, D_B, r // new_len), F32)] * 2
        out_specs += [_full((new_len, D_B, r // new_len))] * 2
    return pl.pallas_call(
        functools.partial(_proj_kernel, mix_block=mix_block, feature_major=feature_major,
                          emit_vn=emit_vn, new_len=new_len),
        grid=(g, r // tm),
        in_specs=[row(D_MODEL), _full((1, D_MODEL)), *w_specs,
                  _full(wm.shape), _full(bm.shape), _full((1, D_A)), _full((1, D_A)),
                  _full((1, D_B)), _full((1, D_B)), _full((D_B, 1)), _full((D_B, 1))],
        out_specs=out_specs,
        out_shape=out_shape,
        scratch_shapes=scratch,
        compiler_params=pltpu.CompilerParams(dimension_semantics=("arbitrary", "arbitrary"),
                                             vmem_limit_bytes=VMEM_LIMIT),
        name="proj",
    )(x, gn, *w_args, wm, bm, gva, goa, gq, gk, gq.reshape(D_B, 1), gk.reshape(D_B, 1))


def _gated_head_norm(o, zb, gob):
    return o * _head_rms_scale(o) * gob * _silu(zb)


V_ROWS = HEAD_DIM + 16
LOOKAHEAD = 1
NEW_PAD = LANES


def _sample_head_mask(t):
    r_i = lax.broadcasted_iota(jnp.int32, (N_HEADS_B * t, D_B), 0)
    c_i = lax.broadcasted_iota(jnp.int32, (N_HEADS_B * t, D_B), 1)
    return (r_i // t) == (c_i // HEAD_DIM)


def _sample_probs(q_ref, kn_ref, ckt_ref, bc_ref, bn_ref):
    t = q_ref.shape[1]
    q_rep = jnp.concatenate([q_ref[0]] * N_HEADS_B, axis=0)
    q_bd = jnp.where(_sample_head_mask(t), q_rep, 0.0).astype(BF16)
    pad = jnp.zeros((NEW_PAD - t, D_B), F32)
    k_new = jnp.concatenate([kn_ref[0], pad], axis=0).astype(BF16)

    s_c = jnp.dot(q_bd.astype(F32), ckt_ref[0], preferred_element_type=F32) + bc_ref[...]
    s_n = lax.dot_general(q_bd, k_new, NT, preferred_element_type=F32) + bn_ref[...]
    m = jnp.maximum(jnp.max(s_c, axis=-1, keepdims=True), jnp.max(s_n, axis=-1, keepdims=True))
    p_c = jnp.exp2(s_c - m)
    p_n = jnp.exp2(s_n - m)
    return p_c, p_n


def _sample_values(p_c, p_n, vn_ref, cvt_ref, zb_ref, gob_ref, b_ref):
    t = vn_ref.shape[1]
    pad = jnp.zeros((NEW_PAD - t, D_B), F32)
    v_new = jnp.concatenate([vn_ref[0], pad], axis=0).astype(BF16)
    l = jnp.sum(p_c, axis=-1, keepdims=True) + jnp.sum(p_n, axis=-1, keepdims=True)
    o = (lax.dot_general(p_c, cvt_ref[0], NT, preferred_element_type=F32)
         + jnp.dot(p_n.astype(BF16), v_new, preferred_element_type=F32)) / l
    o = jnp.where(_sample_head_mask(t), o, 0.0)
    o_tok = o[0:t]
    for hd in range(1, N_HEADS_B):
        o_tok = o_tok + o[hd * t:(hd + 1) * t]
    for p in range(D_B // LANES):
        sl = slice(p * LANES, (p + 1) * LANES)
        b_ref[0, :, sl] = _gated_head_norm(
            o_tok[:, sl], zb_ref[0, :, sl].astype(F32), gob_ref[:, sl]).astype(b_ref.dtype)


SAFE_SPREAD = 100.0


def _attn_kernel(qt_ref, kt_ref, vt_ref, bias_ref, zb_ref, gobcol_ref,
                 sq_ref, skn_ref, svn_ref, ckt_ref, cvt_ref, bc_ref, bn_ref, szb_ref, gob_ref,
                 b_ref, sb_ref, k16, v1, qm, knorm, m_sc, acc_sc):
    i = pl.program_id(1)

    @pl.when(i == 0)
    def _():
        ones = jnp.ones((V_ROWS - HEAD_DIM, TK), BF16)
        ksq_max = jnp.zeros((N_HEADS_B, 1, 1), F32)
        for t in range(k16.shape[0]):
            cols = slice(t * TK, (t + 1) * TK)
            kt = kt_ref[0, :, cols]
            k16[t] = kt.T.astype(BF16)
            ksq = jnp.sum((kt * kt).reshape(N_HEADS_B, HEAD_DIM, TK), axis=1, keepdims=True)
            ksq_max = jnp.maximum(ksq_max, jnp.max(ksq, axis=2, keepdims=True))
            vt = vt_ref[0, :, cols].astype(BF16)
            for h in range(N_HEADS_B):
                v1[t, h, :HEAD_DIM, :] = vt[h * HEAD_DIM:(h + 1) * HEAD_DIM]
                v1[t, h, HEAD_DIM:, :] = ones
        knorm[...] = jnp.broadcast_to(jnp.sqrt(ksq_max), knorm.shape)

    sample_p = _sample_probs(sq_ref, skn_ref, ckt_ref, bc_ref, bn_ref)

    qf = qt_ref[0].astype(F32).reshape(N_HEADS_B, HEAD_DIM, TQ)
    qnorm = jnp.sqrt(jnp.sum(qf * qf, axis=1, keepdims=True))
    spread = qnorm * knorm[...]
    safe = jnp.max(spread) * 2.0 + 4.0 < SAFE_SPREAD
    ref = spread + float(np.log2(len(PATTERNS)))
    per_block = TQ // TK

    row_lo = lax.broadcasted_iota(jnp.int32, (LANES, TQ), 0) < HEAD_DIM
    row_0 = lax.broadcasted_iota(jnp.int32, (LANES, TQ), 0) == 0
    for p in range(D_B // LANES):
        q2 = qt_ref[0, p * LANES:(p + 1) * LANES, :]
        qm[2 * p, :LANES] = jnp.where(row_lo, q2, jnp.zeros_like(q2))
        qm[2 * p + 1, :LANES] = jnp.where(row_lo, jnp.zeros_like(q2), q2)
    for h in range(N_HEADS_B):
        qm[h, LANES:] = jnp.where(row_0, -ref[h], 0.0).astype(BF16)
    key_ones = jnp.ones((TK, LANES), BF16)

    def scores(t, h):
        p = h // 2
        bias = bias_ref[jnp.minimum(per_block * (i + 1) - 1 - t, bias_ref.shape[0] - 1)]
        keys = jnp.concatenate([k16[t, :, p * LANES:(p + 1) * LANES], key_ones], axis=1)
        return jnp.dot(keys, qm[h], preferred_element_type=F32) + bias

    def accumulate_fixed(tiles, first):
        units = [(t, h) for h in range(N_HEADS_B) for t in tiles]
        sts = []
        for h in range(N_HEADS_B):
            p = h // 2
            keys = jnp.concatenate(
                [jnp.concatenate([k16[t, :, p * LANES:(p + 1) * LANES], key_ones], axis=1)
                 for t in tiles], axis=0)
            both = jnp.dot(keys, qm[h], preferred_element_type=F32)
            for n, t in enumerate(tiles):
                bias = bias_ref[jnp.minimum(per_block * (i + 1) - 1 - t, bias_ref.shape[0] - 1)]
                sts.append(both[n * TK:(n + 1) * TK] + bias)
        pv = [0.0] * N_HEADS_B
        for u, (t, h) in enumerate(units):
            st = sts[u]
            if u + 1 < len(units):
                head = jnp.maximum(st[:8], jnp.minimum(sts[u + 1][:8], NEG))
                st = jnp.concatenate([head, st[8:]], axis=0)
            pv[h] = pv[h] + jnp.dot(v1[t, h], jnp.exp2(st).astype(BF16),
                                    preferred_element_type=F32)
        for h in range(N_HEADS_B):
            acc_sc[h] = pv[h] if first else acc_sc[h] + pv[h]

    accumulate_fixed([per_block * i + n for n in range(per_block)], first=True)

    @pl.when(safe)
    def _():
        def trip(j, carry):
            accumulate_fixed([2 * per_block * j + n for n in range(2 * per_block)], first=False)
            return carry

        lax.fori_loop(0, i // 2, trip, 0)

        @pl.when(i % 2 == 1)
        def _():
            accumulate_fixed([per_block * (i - 1) + n for n in range(per_block)], first=False)

    @pl.when(jnp.logical_not(safe))
    def _():
        m_sc[...] = jnp.full(m_sc.shape, NEG, F32)
        acc_sc[...] = jnp.zeros(acc_sc.shape, F32)

        def body(t, carry):
            sts = [scores(t, h) for h in range(N_HEADS_B)]
            tops = [jnp.max(st, axis=0, keepdims=True) for st in sts]
            for h in range(N_HEADS_B):
                floor = functools.reduce(jnp.minimum, tops[h:h + 1 + LOOKAHEAD],
                                         jnp.full_like(tops[0], NEG))
                m_old = m_sc[h]
                m_new = jnp.maximum(jnp.maximum(m_old, tops[h]), floor)
                alpha = jnp.exp2(m_old - m_new)
                pt = jnp.exp2(sts[h] - m_new).astype(BF16)
                acc_sc[h] = alpha * acc_sc[h] + jnp.dot(v1[t, h], pt,
                                                        preferred_element_type=F32)
                m_sc[h] = m_new
            return carry

        lax.fori_loop(0, per_block * (i + 1), body, 0)

    _sample_values(*sample_p, svn_ref, cvt_ref, szb_ref, gob_ref, sb_ref)
    acc = acc_sc[...]
    o = acc[:, :HEAD_DIM] / acc[:, HEAD_DIM:HEAD_DIM + 1]
    ms = jnp.sum(o * o, axis=1, keepdims=True) * (1.0 / HEAD_DIM)
    on = o * lax.rsqrt(ms + EPS) * gobcol_ref[...].reshape(N_HEADS_B, HEAD_DIM, 1)
    b_ref[0] = (on.reshape(D_B, TQ).T * _silu(zb_ref[0].astype(F32))).astype(b_ref.dtype)


def _attn_call(qt, kt, vt, zb, sq, sk_new, sv_new, cache_kt, cache_vt, szb, gob):
    bsz, _, s = qt.shape
    dbs, t, _ = sq.shape
    wb = cache_kt.shape[2]
    nt = s // TQ
    assert dbs == bsz * nt, "one sample sequence per prompt query tile"
    near = max(w for w, _ in PATTERNS[:-1])
    diag = TQ // TK - 1
    nkd = (near + TK - 1) // TK + 2 + diag
    qk = np.arange(TQ)[None, :] - np.arange(TK)[:, None]
    bias = jnp.asarray(np.stack([_log2_bias((kd - diag) * TK + qk) for kd in range(nkd)]))
    tq = np.tile(np.arange(t), N_HEADS_B)[:, None]
    bias_c = jnp.asarray(_log2_bias(wb + tq - np.arange(wb)[None, :]))
    d_new = np.where(np.arange(NEW_PAD)[None, :] < t, tq - np.arange(NEW_PAD)[None, :], -1)
    bias_n = jnp.asarray(_log2_bias(d_new))

    qtile = pl.BlockSpec((1, D_B, TQ), lambda b, i: (b, 0, i))
    tile = pl.BlockSpec((1, TQ, D_B), lambda b, i: (b, i, 0))
    whole = pl.BlockSpec((1, D_B, s), lambda b, i: (b, 0, 0))
    tok = pl.BlockSpec((1, t, D_B), lambda b, i: (b * nt + i, 0, 0))
    win = pl.BlockSpec((1, D_B, wb), lambda b, i: (b * nt + i, 0, 0))
    return pl.pallas_call(
        _attn_kernel,
        grid=(bsz, nt),
        in_specs=[qtile, whole, whole, _full(bias.shape), tile, _full((D_B, 1)),
                  tok, tok, tok, win, win, _full(bias_c.shape), _full(bias_n.shape), tok,
                  _full((1, D_B))],
        out_specs=[tile, tok],
        out_shape=[jax.ShapeDtypeStruct((bsz, s, D_B), BF16),
                   jax.ShapeDtypeStruct((dbs, t, D_B), F32)],
        scratch_shapes=[pltpu.VMEM((s // TK, TK, D_B), BF16),
                        pltpu.VMEM((s // TK, N_HEADS_B, V_ROWS, TK), BF16),
                        pltpu.VMEM((N_HEADS_B, 2 * LANES, TQ), BF16),
                        pltpu.VMEM((N_HEADS_B, 1, TQ), F32),
                        pltpu.VMEM((N_HEADS_B, 1, TQ), F32),
                        pltpu.VMEM((N_HEADS_B, V_ROWS, TQ), F32)],
        compiler_params=pltpu.CompilerParams(dimension_semantics=("parallel", "arbitrary"),
                                             vmem_limit_bytes=VMEM_LIMIT_ATTN),
        name="attn",
    )(qt, kt, vt, bias, zb, gob.reshape(D_B, 1),
      sq, sk_new, sv_new, cache_kt, cache_vt, bias_c, bias_n, szb, gob)


def _finish_kernel(x_ref, a_ref, b_ref, p_ref, wout_ref, wple_ref, wgate_ref, gple_ref, y_ref):
    for r0 in range(0, x_ref.shape[0], FIN_SUB):
        rows = slice(r0, r0 + FIN_SUB)
        h = (x_ref[rows, :]
             + jnp.dot(a_ref[rows, :].astype(BF16), wout_ref[:D_A, :], preferred_element_type=F32)
             + jnp.dot(b_ref[rows, :].astype(BF16), wout_ref[D_A:, :], preferred_element_type=F32))
        e = jnp.dot(p_ref[rows, :].astype(BF16), wple_ref[...], preferred_element_type=F32)
        e = e * _rsqrt_mean(e, D_MODEL) * gple_ref[...]
        h16 = h.astype(BF16)
        for c in range(0, D_MODEL, FIN_CHUNK):
            sl = slice(c, c + FIN_CHUNK)
            gate = jax.nn.sigmoid(jnp.dot(h16, wgate_ref[:, sl], preferred_element_type=F32))
            y_ref[rows, sl] = h[:, sl] + gate * e[:, sl]


def _finish_call(x, a, b, p, wout16, wple16, wgate16, gple):
    n = x.shape[0]
    tm = min(TM_FIN, n)
    row = lambda w: pl.BlockSpec((tm, w), lambda i: (i, 0))
    return pl.pallas_call(
        _finish_kernel,
        grid=(n // tm,),
        in_specs=[row(D_MODEL), row(D_A), row(D_B), row(D_PLE), _full(wout16.shape),
                  _full(wple16.shape), _full(wgate16.shape), _full((1, D_MODEL))],
        out_specs=row(D_MODEL),
        out_shape=jax.ShapeDtypeStruct((n, D_MODEL), F32),
        compiler_params=pltpu.CompilerParams(dimension_semantics=("parallel",),
                                             vmem_limit_bytes=VMEM_LIMIT),
        name="finish",
    )(x, a, b, p, wout16, wple16, wgate16, gple)


def _feature_major(c):
    b, p, h, dh = c.shape
    return jnp.transpose(c, (0, 2, 3, 1)).reshape(b, h * dh, p)


def _position_major(ct, h):
    b, hd, p = ct.shape
    return jnp.transpose(ct.reshape(b, h, hd // h, p), (0, 3, 1, 2))


def kernel(x_prompt, x_sample, cache_k, cache_v, p_prompt, p_sample, g_norm, w_in, w_s, b_s,
           g_va, g_oa, g_q, g_k, g_ob, w_out, w_ple, g_ple, w_ple_gate):
    depth = w_in.shape[0]
    assert depth == 1, "single-layer step"
    bsz, seq, _ = x_prompt.shape
    dbs, dseq, _ = x_sample.shape
    assert seq % TQ == 0 and seq <= max(w for w, _ in PATTERNS) and CHUNK % dseq == 0
    i = 0

    gn = g_norm[i].reshape(1, D_MODEL)
    win16 = w_in[i].astype(BF16)
    gva = g_va[i].reshape(1, D_A)
    goa = g_oa[i].reshape(1, D_A)
    gq = jnp.tile(g_q[i], N_HEADS_B).reshape(1, D_B)
    gk = jnp.tile(g_k[i], N_HEADS_B).reshape(1, D_B)
    gob = g_ob[i].reshape(1, D_B)
    wout16 = w_out[i].astype(BF16)
    wple16 = w_ple[i].astype(BF16)
    wgate16 = w_ple_gate[i].astype(BF16)
    gple = g_ple[i].reshape(1, D_MODEL)

    a_p, qt_p, kt_p, vt_p, zb_p = _proj_call(
        x_prompt, gn, win16, w_in[i], w_s[i], b_s[i][:, :, None], gva, goa, gq, gk,
        mix_block=CHUNK, q_dtype=BF16, feature_major=True, emit_vn=False)
    rep = CHUNK // dseq
    wm_s = jnp.tile(w_s[i][:, :dseq, :dseq], (1, rep, rep))
    bm_s = jnp.tile(b_s[i][:, :dseq], (1, rep))[:, :, None]
    xs = x_sample.reshape(1, dbs * dseq, D_MODEL)
    a_s, q_s, k_s, v_s, zb_s, vn_s, knew, vnew = _proj_call(
        xs, gn, win16, w_in[i], wm_s, bm_s, gva, goa, gq, gk,
        mix_block=dseq, q_dtype=F32, feature_major=False, emit_vn=True, new_len=dseq)

    tok = lambda z: z.reshape(dbs, dseq, D_B)
    b_p, b_s_ = _attn_call(qt_p, kt_p, vt_p, zb_p, tok(q_s), tok(k_s), tok(v_s),
                           _feature_major(cache_k[i]), _feature_major(cache_v[i]), tok(zb_s), gob)

    y_p = _finish_call(x_prompt.reshape(bsz * seq, D_MODEL), a_p.reshape(bsz * seq, D_A),
                       b_p.reshape(bsz * seq, D_B), p_prompt[i].reshape(bsz * seq, D_PLE),
                       wout16, wple16, wgate16, gple)
    y_s = _finish_call(xs.reshape(dbs * dseq, D_MODEL), a_s.reshape(dbs * dseq, D_A),
                       b_s_.reshape(dbs * dseq, D_B), p_sample[i].reshape(dbs * dseq, D_PLE),
                       wout16, wple16, wgate16, gple)

    hshape = (N_HEADS_B, HEAD_DIM)
    return (y_p.reshape(bsz, seq, D_MODEL),
            y_s.reshape(dbs, dseq, D_MODEL),
            _position_major(kt_p, N_HEADS_B)[None],
            _position_major(vt_p, N_HEADS_B)[None],
            jnp.transpose(knew.reshape(dseq, *hshape, dbs), (3, 0, 1, 2))[None],
            jnp.transpose(vnew.reshape(dseq, *hshape, dbs), (3, 0, 1, 2))[None],
            vn_s.reshape(1, dbs, dseq, D_A))
```
